```python
import math
import jax, jax.numpy as jnp
from jax import lax
import numpy as np

D_MODEL = 2048
BATCH = 8
SEQ = 2048
DEPTH = 2

HEAD_DIM = 128
A_HEADS = 12
B_HEADS = 4
MIX_WIDTH = (A_HEADS + B_HEADS) * HEAD_DIM
A_COLS = 3 * A_HEADS * HEAD_DIM
B_COLS = 3 * B_HEADS * HEAD_DIM
DILATED_CONFIGS = ((128, 1), (512, 4), (2048, 16))
MOBA_BLOCK = 256
MOBA_TOP_K = 3
MOBA_QCHUNK = 16
ROPE_THETA = 10000.0
MLA_HEADS = 16
MLA_Q_RANK = 512
MLA_KV_RANK = 512
MLA_NOPE_DIM = 128
MLA_ROPE_DIM = 64
MLA_V_DIM = 128
Q_BLOCK = 128
N_EXPERTS = 32
MOE_TOP_K = 4
D_FF = 2048
SWIGLU_LIMIT = 7.0
SWIGLU_ALPHA = 1.702
ROW_BLOCK = 128
LN_EPS = 1e-5
RMS_EPS = 1e-6

kernel_name = 'hybrid_longnet_moba_mla_moe_deepnorm'

F32 = jnp.float32


def layer_norm(x, g, b):
    xf = x.astype(F32)
    mu = jnp.mean(xf, axis=-1, keepdims=True)
    var = jnp.mean(jnp.square(xf - mu), axis=-1, keepdims=True)
    return ((xf - mu) * lax.rsqrt(var + LN_EPS) * g + b).astype(x.dtype)


def rms_norm(x, g):
    xf = x.astype(F32)
    return (xf * lax.rsqrt(jnp.mean(jnp.square(xf), -1, keepdims=True) + RMS_EPS) * g).astype(x.dtype)


def rope_tables(seq, dim):
    pos = jnp.arange(seq, dtype=F32)
    inv = 1.0 / (ROPE_THETA ** (jnp.arange(0, dim, 2, dtype=F32) / dim))
    ang = pos[:, None] * inv[None, :]
    return jnp.cos(ang), jnp.sin(ang)


def apply_rope(x, cos, sin):
    xf = x.astype(F32)
    x1, x2 = jnp.split(xf, 2, axis=-1)
    c, s = cos[None, :, None, :], sin[None, :, None, :]
    return jnp.concatenate([x1 * c - x2 * s, x2 * c + x1 * s], -1).astype(x.dtype)


def dilated_window_attention(q, k, v, window, dilation):
    B, H, S, hd = q.shape
    band = window // dilation
    L = -(-S // dilation)
    nb = -(-L // band)
    Lp = nb * band
    Sp = Lp * dilation

    def to_blocks(t):
        t = jnp.pad(t, ((0, 0), (0, 0), (0, Sp - S), (0, 0)))
        t = t.reshape(B, H, Lp, dilation, hd).transpose(0, 1, 3, 2, 4)
        return t.reshape(B, H, dilation, nb, band, hd)

    def with_prev(t):
        prev = jnp.pad(t[:, :, :, :-1], ((0, 0), (0, 0), (0, 0), (1, 0), (0, 0), (0, 0)))
        return jnp.concatenate([prev, t], axis=4)

    qs = to_blocks(q)
    kk = with_prev(to_blocks(k))
    vv = with_prev(to_blocks(v))
    s = jnp.einsum('bhrnqd,bhrnkd->bhrnqk', qs, kk).astype(F32) * (hd ** -0.5)
    qi = jnp.arange(band)[:, None]
    kj = jnp.arange(2 * band)[None, :]
    in_band = (kj >= qi) & (kj <= qi + band)
    has_prev = (jnp.arange(nb) > 0)[:, None, None] | (kj >= band)[None]
    mask = in_band[None] & has_prev
    s = jnp.where(mask, s, -jnp.inf)
    lse = jax.nn.logsumexp(s, axis=-1)
    p = jnp.exp(s - lse[..., None])
    o = jnp.einsum('bhrnqk,bhrnkd->bhrnqd', p.astype(v.dtype), vv)

    def from_blocks(t):
        rest = t.shape[5:]
        t = t.reshape(B, H, dilation, Lp, *rest)
        t = jnp.moveaxis(t, 2, 3).reshape(B, H, Sp, *rest)
        return t[:, :, :S]

    return from_blocks(o), from_blocks(lse)


def longnet_mixture(q, k, v):
    outs, lses = [], []
    for window, dilation in DILATED_CONFIGS:
        o, l = dilated_window_attention(q, k, v, window, dilation)
        outs.append(o)
        lses.append(l)
    w = jax.nn.softmax(jnp.stack(lses, 0), axis=0)
    return jnp.sum(w[..., None] * jnp.stack(outs, 0).astype(F32), axis=0)


def moba_attention(q, k, v):
    B, H, S, hd = q.shape
    nblk = -(-S // MOBA_BLOCK)
    Sp = nblk * MOBA_BLOCK
    padk = ((0, 0), (0, 0), (0, Sp - S), (0, 0))
    kb = jnp.pad(k, padk).reshape(B, H, nblk, MOBA_BLOCK, hd)
    vb = jnp.pad(v, padk).reshape(B, H, nblk, MOBA_BLOCK, hd)
    k_mean = jnp.mean(kb.astype(F32), axis=3)
    gate = jnp.einsum('bhsd,bhnd->bhsn', q.astype(F32), k_mean)
    pos = jnp.arange(S)
    q_blk = pos // MOBA_BLOCK
    fully_past = jnp.arange(nblk)[None, :] < q_blk[:, None]
    gate = jnp.where(fully_past, gate, -jnp.inf)
    n_top = min(MOBA_TOP_K, nblk)
    _, top_blk = lax.top_k(gate, n_top)
    own = jnp.broadcast_to(q_blk[:, None], (B, H, S, 1)).astype(top_blk.dtype)
    sel = jnp.concatenate([top_blk, own], -1)
    slot_ok = jnp.concatenate([jnp.arange(n_top)[None, :] < q_blk[:, None],
                               jnp.ones((S, 1), bool)], -1)
    n_sel = n_top + 1
    nc = S // MOBA_QCHUNK

    def chunks(t):
        t = t.reshape(B, H, nc, MOBA_QCHUNK, *t.shape[3:])
        return jnp.moveaxis(t, 2, 0)

    b_ix = jnp.arange(B)[:, None, None, None]
    h_ix = jnp.arange(H)[None, :, None, None]
    offs = jnp.arange(MOBA_BLOCK)
    scale = hd ** -0.5

    def one_chunk(args):
        qc, selc, posc, okc = args
        kg = kb[b_ix, h_ix, selc]
        vg = vb[b_ix, h_ix, selc]
        s = jnp.einsum('bhqd,bhqnkd->bhqnk', qc, kg).astype(F32) * scale
        key_pos = selc[..., None] * MOBA_BLOCK + offs
        mask = okc[None, None, :, :, None] & (key_pos <= posc[None, None, :, None, None])
        s = jnp.where(mask, s, -jnp.inf).reshape(B, H, MOBA_QCHUNK, n_sel * MOBA_BLOCK)
        p = jax.nn.softmax(s, axis=-1).reshape(B, H, MOBA_QCHUNK, n_sel, MOBA_BLOCK)
        return jnp.einsum('bhqnk,bhqnkd->bhqd', p.astype(v.dtype), vg)

    out = lax.map(one_chunk, (chunks(q), chunks(sel), pos.reshape(nc, MOBA_QCHUNK),
                              slot_ok.reshape(nc, MOBA_QCHUNK, n_sel)))
    return jnp.moveaxis(out, 0, 2).reshape(B, H, S, hd)


def mixer_ab(x, w_in, w_out, cos, sin):
    B, S, _ = x.shape
    proj = x @ w_in
    a_part, b_part = proj[..., :A_COLS], proj[..., A_COLS:]
    qa, ka, va = jnp.moveaxis(a_part.reshape(B, S, 3, A_HEADS, HEAD_DIM), 2, 0)
    qb, kb, vb = jnp.moveaxis(b_part.reshape(B, S, 3, B_HEADS, HEAD_DIM), 2, 0)
    bhsd = lambda t: t.transpose(0, 2, 1, 3)
    qa, ka = bhsd(apply_rope(qa, cos, sin)), bhsd(apply_rope(ka, cos, sin))
    qb, kb = bhsd(apply_rope(qb, cos, sin)), bhsd(apply_rope(kb, cos, sin))
    o_a = longnet_mixture(qa, ka, bhsd(va)).astype(x.dtype)
    o_b = moba_attention(qb, kb, bhsd(vb)).astype(x.dtype)
    o = jnp.concatenate([o_a, o_b], axis=1).transpose(0, 2, 1, 3).reshape(B, S, MIX_WIDTH)
    return o @ w_out


def causal_attention_qblocks(q, k, v):
    B, H, S, dqk = q.shape
    nq = S // Q_BLOCK
    scale = dqk ** -0.5
    key_pos = jnp.arange(S)
    qb = jnp.moveaxis(q.reshape(B, H, nq, Q_BLOCK, dqk), 2, 0)

    def one_block(args):
        qblk, start = args
        s = jnp.einsum('bhqd,bhkd->bhqk', qblk, k).astype(F32) * scale
        q_pos = start + jnp.arange(Q_BLOCK)
        s = jnp.where(key_pos[None, :] <= q_pos[:, None], s, -jnp.inf)
        p = jax.nn.softmax(s, axis=-1)
        return jnp.einsum('bhqk,bhkd->bhqd', p.astype(v.dtype), v)

    out = lax.map(one_block, (qb, jnp.arange(nq) * Q_BLOCK))
    return jnp.moveaxis(out, 0, 2).reshape(B, H, S, v.shape[-1])


def mla_mixer(x, w_dq, q_norm, w_uq, w_dkv, kv_norm, w_ukv, w_out, cos, sin):
    B, S, _ = x.shape
    c_q = rms_norm(x @ w_dq, q_norm)
    q = (c_q @ w_uq).reshape(B, S, MLA_HEADS, MLA_NOPE_DIM + MLA_ROPE_DIM)
    q_nope, q_rope = q[..., :MLA_NOPE_DIM], q[..., MLA_NOPE_DIM:]
    kv_a = x @ w_dkv
    c_kv = rms_norm(kv_a[..., :MLA_KV_RANK], kv_norm)
    k_rope = apply_rope(kv_a[..., None, MLA_KV_RANK:], cos, sin)
    kv = (c_kv @ w_ukv).reshape(B, S, MLA_HEADS, MLA_NOPE_DIM + MLA_V_DIM)
    k_nope, v = kv[..., :MLA_NOPE_DIM], kv[..., MLA_NOPE_DIM:]
    q = jnp.concatenate([q_nope, apply_rope(q_rope, cos, sin)], -1)
    k = jnp.concatenate([k_nope, jnp.broadcast_to(k_rope, (B, S, MLA_HEADS, MLA_ROPE_DIM))], -1)
    bhsd = lambda t: t.transpose(0, 2, 1, 3)
    o = causal_attention_qblocks(bhsd(q), bhsd(k), bhsd(v))
    return o.transpose(0, 2, 1, 3).reshape(B, S, MLA_HEADS * MLA_V_DIM) @ w_out


def clamped_swiglu(h):
    glu, lin = h[..., 0::2], h[..., 1::2]
    glu = jnp.minimum(glu, SWIGLU_LIMIT)
    lin = jnp.clip(lin, -SWIGLU_LIMIT, SWIGLU_LIMIT)
    return glu * jax.nn.sigmoid(SWIGLU_ALPHA * glu) * (lin + 1.0)


def moe_ffn(x, router_w, router_b, w_gu, b_gu, w_down, b_down, layer):
    B, S, D = x.shape
    T = B * S
    xt = x.reshape(T, D)
    logits = (xt @ router_w + router_b).astype(F32)
    top_logit, top_e = lax.top_k(logits, MOE_TOP_K)
    gates = jax.nn.softmax(top_logit, axis=-1)
    flat_e = top_e.reshape(-1)
    n_assign = T * MOE_TOP_K
    order = jnp.argsort(flat_e)
    sorted_e = flat_e[order]
    counts = jnp.bincount(flat_e, length=N_EXPERTS)
    padded = (counts + ROW_BLOCK - 1) // ROW_BLOCK * ROW_BLOCK
    start = jnp.cumsum(counts) - counts
    pad_end = jnp.cumsum(padded)
    pad_start = pad_end - padded
    dest = pad_start[sorted_e] + jnp.arange(n_assign) - start[sorted_e]
    n_rows = n_assign + N_EXPERTS * ROW_BLOCK
    n_blocks = n_rows // ROW_BLOCK
    row_token = jnp.full((n_rows,), T, jnp.int32).at[dest].set((order // MOE_TOP_K).astype(jnp.int32))
    row_gate = jnp.zeros((n_rows,), F32).at[dest].set(gates.reshape(-1)[order])
    block_expert = jnp.minimum(
        jnp.searchsorted(pad_end, jnp.arange(n_blocks) * ROW_BLOCK, side='right'), N_EXPERTS - 1)
    x_pad = jnp.concatenate([xt, jnp.zeros((1, D), xt.dtype)], 0)
    xs = x_pad[row_token].reshape(n_blocks, ROW_BLOCK, D)

    def expert_block(args):
        xb, e = args
        h = xb @ w_gu[layer, e] + b_gu[layer, e]
        return clamped_swiglu(h) @ w_down[layer, e] + b_down[layer, e]

    ys = lax.map(expert_block, (xs, block_expert)).reshape(n_rows, D)
    y = jnp.zeros((T + 1, D), ys.dtype).at[row_token].add(row_gate[:, None].astype(ys.dtype) * ys)
    return y[:T].reshape(B, S, D)


def setup_inputs(seed: int = 0) -> dict:
    key = jax.random.key(seed)
    ks = jax.random.split(key, 32)
    n_even = (DEPTH + 1) // 2
    n_odd = DEPTH // 2
    beta = (8.0 * DEPTH) ** -0.25
    D = D_MODEL

    def nrm(k, shape, scale):
        return jax.random.normal(k, shape, F32) * scale

    return {
        'x': nrm(ks[0], (BATCH, SEQ, D), 1.0),
        'ln_mix_g': 1.0 + nrm(ks[1], (DEPTH, D), 0.02),
        'ln_mix_b': nrm(ks[2], (DEPTH, D), 0.02),
        'ln_ffn_g': 1.0 + nrm(ks[3], (DEPTH, D), 0.02),
        'ln_ffn_b': nrm(ks[4], (DEPTH, D), 0.02),
        'w_in_ab': nrm(ks[5], (n_even, D, A_COLS + B_COLS), D ** -0.5),
        'w_out_ab': nrm(ks[6], (n_even, MIX_WIDTH, D), beta * MIX_WIDTH ** -0.5),
        'mla_w_dq': nrm(ks[7], (n_odd, D, MLA_Q_RANK), D ** -0.5),
        'mla_q_norm': 1.0 + nrm(ks[8], (n_odd, MLA_Q_RANK), 0.02),
        'mla_w_uq': nrm(ks[9], (n_odd, MLA_Q_RANK, MLA_HEADS * (MLA_NOPE_DIM + MLA_ROPE_DIM)), MLA_Q_RANK ** -0.5),
        'mla_w_dkv': nrm(ks[10], (n_odd, D, MLA_KV_RANK + MLA_ROPE_DIM), D ** -0.5),
        'mla_kv_norm': 1.0 + nrm(ks[11], (n_odd, MLA_KV_RANK), 0.02),
        'mla_w_ukv': nrm(ks[12], (n_odd, MLA_KV_RANK, MLA_HEADS * (MLA_NOPE_DIM + MLA_V_DIM)), MLA_KV_RANK ** -0.5),
        'mla_w_out': nrm(ks[13], (n_odd, MLA_HEADS * MLA_V_DIM, D), beta * (MLA_HEADS * MLA_V_DIM) ** -0.5),
        'router_w': nrm(ks[14], (DEPTH, D, N_EXPERTS), D ** -0.5),
        'router_b': nrm(ks[15], (DEPTH, N_EXPERTS), 0.01),
        'moe_w_gu': nrm(ks[16], (DEPTH, N_EXPERTS, D, 2 * D_FF), D ** -0.5),
        'moe_b_gu': nrm(ks[17], (DEPTH, N_EXPERTS, 2 * D_FF), 0.02),
        'moe_w_down': nrm(ks[18], (DEPTH, N_EXPERTS, D_FF, D), beta * D_FF ** -0.5),
        'moe_b_down': nrm(ks[19], (DEPTH, N_EXPERTS, D), 0.02),
    }


def reference(x, ln_mix_g, ln_mix_b, ln_ffn_g, ln_ffn_b, w_in_ab, w_out_ab,
              mla_w_dq, mla_q_norm, mla_w_uq, mla_w_dkv, mla_kv_norm, mla_w_ukv, mla_w_out,
              router_w, router_b, moe_w_gu, moe_b_gu, moe_w_down, moe_b_down):
    S = x.shape[1]
    alpha = (2.0 * DEPTH) ** 0.25
    cos_h, sin_h = rope_tables(S, HEAD_DIM)
    cos_r, sin_r = rope_tables(S, MLA_ROPE_DIM)
    for layer in range(DEPTH):
        i = layer // 2
        if layer % 2 == 0:
            h = mixer_ab(x, w_in_ab[i], w_out_ab[i], cos_h, sin_h)
        else:
            h = mla_mixer(x, mla_w_dq[i], mla_q_norm[i], mla_w_uq[i], mla_w_dkv[i],
                          mla_kv_norm[i], mla_w_ukv[i], mla_w_out[i], cos_r, sin_r)
        x = layer_norm(alpha * x + h, ln_mix_g[layer], ln_mix_b[layer])
        h = moe_ffn(x, router_w[layer], router_b[layer], moe_w_gu, moe_b_gu,
                    moe_w_down, moe_b_down, layer)
        x = layer_norm(alpha * x + h, ln_ffn_g[layer], ln_ffn_b[layer])
    return x
```

```python
import functools

import jax
import jax.numpy as jnp
from jax import lax
from jax.experimental import pallas as pl
from jax.experimental.pallas import tpu as pltpu

F32 = jnp.float32
BF16 = jnp.bfloat16

DEPTH = 2
HEAD_DIM = 128
A_HEADS = 12
B_HEADS = 4
DILATED_CONFIGS = ((128, 1), (512, 4), (2048, 16))
MOBA_BLOCK = 256
MOBA_TOP_K = 3
ROPE_THETA = 10000.0
MLA_HEADS = 16
MLA_Q_RANK = 512
MLA_KV_RANK = 512
MLA_NOPE_DIM = 128
MLA_ROPE_DIM = 64
MLA_V_DIM = 128
N_EXPERTS = 32
MOE_TOP_K = 4
SWIGLU_LIMIT = 7.0
SWIGLU_ALPHA = 1.702
LN_EPS = 1e-5
RMS_EPS = 1e-6
ALPHA = (2.0 * DEPTH) ** 0.25

LANES = 128
MASK_VALUE = -1e30
VMEM_LIMIT = 56 * 1024 * 1024

MM_TILE_M = 512
MM_TILE_N = 1024
LN_TILE_M = 256
ATT_TILE_Q = 256
MOE_TILE_M = 512
MOE_TILE_F = 1024
COMBINE_TILE = 128


def _params(*semantics):
    return pltpu.CompilerParams(dimension_semantics=semantics,
                                vmem_limit_bytes=VMEM_LIMIT)


def _dot(a, b):
    return jnp.dot(a, b, preferred_element_type=F32)


def _dot_nt(a, b):
    return lax.dot_general(a, b, (((1,), (1,)), ((), ())),
                           preferred_element_type=F32)


def _dot_nt_f32(a, b):
    return lax.dot_general(a, b, (((1,), (1,)), ((), ())),
                           precision=lax.Precision.HIGHEST,
                           preferred_element_type=F32)


def _rope(x, cos, sin):
    return x * cos + pltpu.roll(x, LANES // 2, 1) * sin


def _mm_kernel(x_ref, w_ref, o_ref, xb_ref):
    @pl.when(pl.program_id(1) == 0)
    def _():
        xb_ref[...] = x_ref[...].astype(BF16)
    o_ref[...] = _dot(xb_ref[...], w_ref[...])


def _matmul(x, w):
    m, k = x.shape
    n = w.shape[1]
    tm, tn = min(MM_TILE_M, m), min(MM_TILE_N, n)
    return pl.pallas_call(
        _mm_kernel,
        out_shape=jax.ShapeDtypeStruct((m, n), F32),
        grid=(m // tm, n // tn),
        in_specs=[pl.BlockSpec((tm, k), lambda i, j: (i, 0)),
                  pl.BlockSpec((k, tn), lambda i, j: (0, j))],
        out_specs=pl.BlockSpec((tm, tn), lambda i, j: (i, j)),
        scratch_shapes=[pltpu.VMEM((tm, k), BF16)],
        compiler_params=_params("parallel", "arbitrary"),
        name="proj_matmul",
    )(x, w)


def _longnet_kernel(q_ref, k_ref, v_ref, cos_ref, sin_ref, o_ref,
                    qs, ks, vs, oc, lc, *, seq):
    band = DILATED_CONFIGS[0][0] // DILATED_CONFIGS[0][1]
    cos, sin = cos_ref[...], sin_ref[...]
    qs[...] = _rope(q_ref[0], cos, sin) * (HEAD_DIM ** -0.5)
    ks[...] = _rope(k_ref[0], cos, sin)
    vs[...] = v_ref[0]
    row = lax.broadcasted_iota(jnp.int32, (band, band), 0)
    col = lax.broadcasted_iota(jnp.int32, (band, band), 1)
    own_mask = col <= row
    prev_mask = col >= row

    for c, (window, dil) in enumerate(DILATED_CONFIGS):
        assert window // dil == band and seq % (dil * band) == 0
        nb = seq // (dil * band)
        for r in range(dil):
            for n in range(nb):
                def rows(start):
                    if dil == 1:
                        return pl.ds(start, band)
                    return pl.ds(start, band, stride=dil)
                cur = rows(r + dil * band * n)
                qb = qs[cur, :].astype(BF16)
                kb = ks[cur, :].astype(BF16)
                vb = vs[cur, :].astype(BF16)
                s_own = jnp.where(own_mask, _dot_nt(qb, kb), MASK_VALUE)
                m = jnp.max(s_own, axis=-1, keepdims=True)
                if n > 0:
                    prev = rows(r + dil * band * (n - 1))
                    kp = ks[prev, :].astype(BF16)
                    vp = vs[prev, :].astype(BF16)
                    s_prev = jnp.where(prev_mask, _dot_nt(qb, kp), MASK_VALUE)
                    m = jnp.maximum(m, jnp.max(s_prev, axis=-1, keepdims=True))
                p_own = jnp.exp(s_own - m)
                l = jnp.sum(p_own, axis=-1, keepdims=True)
                acc = _dot(p_own.astype(BF16), vb)
                if n > 0:
                    p_prev = jnp.exp(s_prev - m)
                    l = l + jnp.sum(p_prev, axis=-1, keepdims=True)
                    acc = acc + _dot(p_prev.astype(BF16), vp)
                oc[c, cur, :] = acc / l
                lc[c, cur, :] = jnp.broadcast_to(m + jnp.log(l), (band, LANES))

    lses = [lc[c] for c in range(len(DILATED_CONFIGS))]
    top = functools.reduce(jnp.maximum, lses)
    es = [jnp.exp(v - top) for v in lses]
    num = sum(e * oc[c] for c, e in enumerate(es))
    o_ref[0] = num / sum(es)


def _longnet(proj, cos, sin):
    b, s, _ = proj.shape
    n_cfg = len(DILATED_CONFIGS)
    blk = lambda off: pl.BlockSpec((1, s, HEAD_DIM), lambda i, h: (i, 0, off + h))
    tab = pl.BlockSpec((s, HEAD_DIM), lambda i, h: (0, 0))
    return pl.pallas_call(
        functools.partial(_longnet_kernel, seq=s),
        out_shape=jax.ShapeDtypeStruct((b, s, A_HEADS * HEAD_DIM), F32),
        grid=(b, A_HEADS),
        in_specs=[blk(0), blk(A_HEADS), blk(2 * A_HEADS), tab, tab],
        out_specs=pl.BlockSpec((1, s, HEAD_DIM), lambda i, h: (i, 0, h)),
        scratch_shapes=[pltpu.VMEM((s, HEAD_DIM), F32),
                        pltpu.VMEM((s, HEAD_DIM), F32),
                        pltpu.VMEM((s, HEAD_DIM), F32),
                        pltpu.VMEM((n_cfg, s, HEAD_DIM), F32),
                        pltpu.VMEM((n_cfg, s, LANES), F32)],
        compiler_params=_params("parallel", "parallel"),
        name="longnet_attention",
    )(proj, proj, proj, cos, sin)


def _softmax_blocks(qb, k_ref, v_ref, blocks, tile):
    scores = []
    m = None
    for start, mask in blocks:
        s = _dot_nt(qb, k_ref[pl.ds(start, tile), :])
        if mask is not None:
            s = jnp.where(mask, s, MASK_VALUE)
        scores.append(s)
        bm = jnp.max(s, axis=-1, keepdims=True)
        m = bm if m is None else jnp.maximum(m, bm)
    l = None
    acc = None
    for (start, _), s in zip(blocks, scores):
        p = jnp.exp(s - m)
        pl_sum = jnp.sum(p, axis=-1, keepdims=True)
        pv = _dot(p.astype(BF16), v_ref[pl.ds(start, tile), :])
        l = pl_sum if l is None else l + pl_sum
        acc = pv if acc is None else acc + pv
    return acc / l


def _moba_kernel(q_ref, k_ref, v_ref, cos_ref, sin_ref, o_ref, qs, ks, vs, *, seq):
    blk = MOBA_BLOCK
    nblk = seq // blk
    cos, sin = cos_ref[...], sin_ref[...]
    q = _rope(q_ref[0], cos, sin)
    k = _rope(k_ref[0], cos, sin)
    qs[...] = (q * (HEAD_DIM ** -0.5)).astype(BF16)
    ks[...] = k.astype(BF16)
    vs[...] = v_ref[0].astype(BF16)
    k_mean = jnp.concatenate(
        [jnp.mean(k[n * blk:(n + 1) * blk], axis=0, keepdims=True) for n in range(nblk)],
        axis=0)
    gate = _dot_nt_f32(q, k_mean)
    row = lax.broadcasted_iota(jnp.int32, (blk, blk), 0)
    col = lax.broadcasted_iota(jnp.int32, (blk, blk), 1)
    causal = col <= row
    n_top = min(MOBA_TOP_K, nblk)

    for i in range(nblk):
        g = gate[i * blk:(i + 1) * blk, :]
        blocks = [(i * blk, causal)]
        for n in range(i):
            if i <= n_top:
                blocks.append((n * blk, None))
                continue
            gn = g[:, n:n + 1]
            rank = jnp.zeros((blk, 1), jnp.int32)
            for o in range(i):
                if o == n:
                    continue
                go = g[:, o:o + 1]
                ahead = (go > gn) | (go == gn) if o < n else (go > gn)
                rank = rank + ahead.astype(jnp.int32)
            blocks.append((n * blk, rank < n_top))
        o_ref[0, pl.ds(i * blk, blk), :] = _softmax_blocks(
            qs[pl.ds(i * blk, blk), :], ks, vs, blocks, blk)


def _moba(proj, cos, sin):
    b, s, _ = proj.shape
    base = 3 * A_HEADS
    blk = lambda off: pl.BlockSpec((1, s, HEAD_DIM), lambda i, h: (i, 0, base + off + h))
    tab = pl.BlockSpec((s, HEAD_DIM), lambda i, h: (0, 0))
    return pl.pallas_call(
        functools.partial(_moba_kernel, seq=s),
        out_shape=jax.ShapeDtypeStruct((b, s, B_HEADS * HEAD_DIM), F32),
        grid=(b, B_HEADS),
        in_specs=[blk(0), blk(B_HEADS), blk(2 * B_HEADS), tab, tab],
        out_specs=pl.BlockSpec((1, s, HEAD_DIM), lambda i, h: (i, 0, h)),
        scratch_shapes=[pltpu.VMEM((s, HEAD_DIM), BF16)] * 3,
        compiler_params=_params("parallel", "parallel"),
        name="moba_attention",
    )(proj, proj, proj, cos, sin)


def _layer_norm(z, g, b):
    mu = jnp.mean(z, axis=-1, keepdims=True)
    zc = z - mu
    var = jnp.mean(zc * zc, axis=-1, keepdims=True)
    return zc * lax.rsqrt(var + LN_EPS) * g + b


def _route_topk(y, rw_ref, rb_ref, e_ref, g_ref):
    logits = jnp.dot(y, rw_ref[...], precision=lax.Precision.HIGHEST,
                     preferred_element_type=F32) + rb_ref[...]
    lane = lax.broadcasted_iota(jnp.int32, logits.shape, 1)
    vals, idxs = [], []
    for _ in range(MOE_TOP_K):
        v = jnp.max(logits, axis=-1, keepdims=True)
        ix = jnp.min(jnp.where(logits == v, lane, LANES), axis=-1, keepdims=True)
        vals.append(v)
        idxs.append(ix)
        logits = jnp.where(lane == ix, -jnp.inf, logits)
    exps = [jnp.exp(v - vals[0]) for v in vals]
    denom = sum(exps)
    e_out = jnp.zeros(lane.shape, jnp.int32)
    g_out = jnp.zeros(lane.shape, F32)
    for kk in range(MOE_TOP_K):
        e_out = jnp.where(lane == kk, idxs[kk], e_out)
        g_out = jnp.where(lane == kk, exps[kk] / denom, g_out)
    e_ref[...] = e_out
    g_ref[...] = g_out


def _outproj_ln_kernel(*refs, n_in):
    o_refs, w_refs = refs[:n_in], refs[n_in:2 * n_in]
    x_ref, g_ref, b_ref, rw_ref, rb_ref, y_ref, e_ref, gt_ref = refs[2 * n_in:]
    h = sum(_dot(o[...].astype(BF16), w[...]) for o, w in zip(o_refs, w_refs))
    y = _layer_norm(ALPHA * x_ref[...] + h, g_ref[...], b_ref[...])
    y_ref[...] = y
    _route_topk(y, rw_ref, rb_ref, e_ref, gt_ref)


def _outproj_ln(outs, ws, x, g, b, rw, rb):
    t, d = x.shape
    tm = LN_TILE_M
    n_in = len(outs)
    row = lambda width: pl.BlockSpec((tm, width), lambda i: (i, 0))
    full = lambda a: pl.BlockSpec(a.shape, lambda i: (0,) * a.ndim)
    return pl.pallas_call(
        functools.partial(_outproj_ln_kernel, n_in=n_in),
        out_shape=(jax.ShapeDtypeStruct((t, d), F32),
                   jax.ShapeDtypeStruct((t, LANES), jnp.int32),
                   jax.ShapeDtypeStruct((t, LANES), F32)),
        grid=(t // tm,),
        in_specs=([row(o.shape[1]) for o in outs] + [full(w) for w in ws]
                  + [row(d), full(g), full(b), full(rw), full(rb)]),
        out_specs=(row(d), row(LANES), row(LANES)),
        compiler_params=_params("parallel"),
        name="outproj_layernorm_router",
    )(*outs, *ws, x, g, b, rw, rb)


def _rms_norm(x, g):
    return x * lax.rsqrt(jnp.mean(x * x, axis=-1, keepdims=True) + RMS_EPS) * g


def _mla_down_kernel(x_ref, w_ref, qn_ref, kvn_ref, cq_ref, ckv_ref, kr_ref):
    a = _dot(x_ref[...].astype(BF16), w_ref[...])
    cq_ref[...] = _rms_norm(a[:, :MLA_Q_RANK], qn_ref[...]).astype(BF16)
    ckv_ref[...] = _rms_norm(a[:, MLA_Q_RANK:MLA_Q_RANK + MLA_KV_RANK],
                             kvn_ref[...]).astype(BF16)
    kr_ref[...] = a[:, MLA_Q_RANK + MLA_KV_RANK:]


def _mla_down(x, w_cat, q_norm, kv_norm):
    t, d = x.shape
    tm = MM_TILE_M
    row = lambda width: pl.BlockSpec((tm, width), lambda i: (i, 0))
    full = lambda a: pl.BlockSpec(a.shape, lambda i: (0,) * a.ndim)
    return pl.pallas_call(
        _mla_down_kernel,
        out_shape=(jax.ShapeDtypeStruct((t, MLA_Q_RANK), BF16),
                   jax.ShapeDtypeStruct((t, MLA_KV_RANK), BF16),
                   jax.ShapeDtypeStruct((t, LANES), F32)),
        grid=(t // tm,),
        in_specs=[row(d), full(w_cat), full(q_norm), full(kv_norm)],
        out_specs=(row(MLA_Q_RANK), row(MLA_KV_RANK), row(LANES)),
        compiler_params=_params("parallel"),
        name="mla_down_rmsnorm",
    )(x, w_cat, q_norm, kv_norm)


def _mla_attn_kernel(cq_ref, ckv_ref, kr_ref, wq_ref, wkv_ref, cos_ref, sin_ref,
                     o_ref, qs, ks, vs, *, seq):
    tile = ATT_TILE_Q
    scale = (MLA_NOPE_DIM + MLA_ROPE_DIM) ** -0.5
    cos, sin = cos_ref[...], sin_ref[...]
    q = _dot(cq_ref[0], wq_ref[...])
    kv = _dot(ckv_ref[0], wkv_ref[...])
    qs[:, :LANES] = (q[:, :LANES] * scale).astype(BF16)
    qs[:, LANES:] = (_rope(q[:, LANES:], cos, sin) * scale).astype(BF16)
    ks[:, :LANES] = kv[:, :MLA_NOPE_DIM].astype(BF16)
    ks[:, LANES:] = _rope(kr_ref[0], cos, sin).astype(BF16)
    vs[...] = kv[:, MLA_NOPE_DIM:].astype(BF16)
    row = lax.broadcasted_iota(jnp.int32, (tile, tile), 0)
    col = lax.broadcasted_iota(jnp.int32, (tile, tile), 1)
    causal = col <= row
    for i in range(seq // tile):
        blocks = [(i * tile, causal)] + [(n * tile, None) for n in range(i)]
        o_ref[0, pl.ds(i * tile, tile), :] = _softmax_blocks(
            qs[pl.ds(i * tile, tile), :], ks, vs, blocks, tile)


def _mla_attn(cq, ckv, kr, wq, wkv, cos, sin):
    b, s, _ = cq.shape
    per_b = lambda width: pl.BlockSpec((1, s, width), lambda i, h: (i, 0, 0))
    head_w = lambda a: pl.BlockSpec((a.shape[0], 2 * LANES), lambda i, h: (0, h))
    tab = pl.BlockSpec((s, LANES), lambda i, h: (0, 0))
    return pl.pallas_call(
        functools.partial(_mla_attn_kernel, seq=s),
        out_shape=jax.ShapeDtypeStruct((b, s, MLA_HEADS * MLA_V_DIM), F32),
        grid=(b, MLA_HEADS),
        in_specs=[per_b(MLA_Q_RANK), per_b(MLA_KV_RANK), per_b(LANES),
                  head_w(wq), head_w(wkv), tab, tab],
        out_specs=pl.BlockSpec((1, s, MLA_V_DIM), lambda i, h: (i, 0, h)),
        scratch_shapes=[pltpu.VMEM((s, 2 * LANES), BF16),
                        pltpu.VMEM((s, 2 * LANES), BF16),
                        pltpu.VMEM((s, MLA_V_DIM), BF16)],
        compiler_params=_params("parallel", "arbitrary"),
        name="mla_attention",
    )(cq, ckv, kr, wq, wkv, cos, sin)


def _gather_rows(idx_hbm_row, idx_smem, src_hbm, n, sem_idx, sem_rows, dst_of):
    cp = pltpu.make_async_copy(idx_hbm_row, idx_smem, sem_idx)
    cp.start()
    cp.wait()

    def issue(a, carry):
        pltpu.make_async_copy(src_hbm.at[pl.ds(idx_smem[a], 1)], dst_of(a),
                              sem_rows).start()
        return carry
    lax.fori_loop(0, n, issue, 0)


def _moe_kernel(te_ref, nu_ref, tok_hbm, x_hbm, wg_ref, wl_ref, bg_ref, bl_ref,
                wd_ref, bd_ref, o_ref, tok_smem, xbuf, xb, sem_idx, sem_rows):
    t, j = pl.program_id(0), pl.program_id(1)
    tm = xbuf.shape[0]
    used = t < nu_ref[0]

    @pl.when(jnp.logical_and(used, j == 0))
    def _():
        _gather_rows(tok_hbm.at[t], tok_smem, x_hbm, tm, sem_idx, sem_rows,
                     lambda a: xbuf.at[pl.ds(a, 1)])
        pltpu.make_async_copy(x_hbm.at[pl.ds(0, tm)], xbuf, sem_rows).wait()
        xb[...] = xbuf[...].astype(BF16)
        o_ref[...] = jnp.broadcast_to(bd_ref[0], o_ref.shape)

    @pl.when(jnp.logical_and(jnp.logical_not(used), j == 0))
    def _():
        o_ref[...] = jnp.zeros(o_ref.shape, F32)

    @pl.when(used)
    def _():
        x = xb[...]
        glu = jnp.minimum(_dot(x, wg_ref[0]) + bg_ref[0], SWIGLU_LIMIT)
        lin = jnp.clip(_dot(x, wl_ref[0]) + bl_ref[0], -SWIGLU_LIMIT, SWIGLU_LIMIT)
        act = glu * jax.nn.sigmoid(SWIGLU_ALPHA * glu) * (lin + 1.0)
        o_ref[...] += _dot(act.astype(BF16), wd_ref[0])


def _moe_experts(x, tok_tiles, tile_expert, n_used, wg, wl, bg, bl, wd, bd):
    t, d = x.shape
    n_tiles, tm = tok_tiles.shape
    ff = wg.shape[2]
    tf = MOE_TILE_F
    grid_spec = pltpu.PrefetchScalarGridSpec(
        num_scalar_prefetch=2,
        grid=(n_tiles, ff // tf),
        in_specs=[
            pl.BlockSpec(memory_space=pl.ANY),
            pl.BlockSpec(memory_space=pl.ANY),
            pl.BlockSpec((1, d, tf), lambda i, j, te, nu: (te[i], 0, j)),
            pl.BlockSpec((1, d, tf), lambda i, j, te, nu: (te[i], 0, j)),
            pl.BlockSpec((1, 1, tf), lambda i, j, te, nu: (te[i], 0, j)),
            pl.BlockSpec((1, 1, tf), lambda i, j, te, nu: (te[i], 0, j)),
            pl.BlockSpec((1, tf, d), lambda i, j, te, nu: (te[i], j, 0)),
            pl.BlockSpec((1, 1, d), lambda i, j, te, nu: (te[i], 0, 0)),
        ],
        out_specs=pl.BlockSpec((tm, d), lambda i, j, te, nu: (i, 0)),
        scratch_shapes=[pltpu.SMEM((tm,), jnp.int32),
                        pltpu.VMEM((tm, d), F32),
                        pltpu.VMEM((tm, d), BF16),
                        pltpu.SemaphoreType.DMA,
                        pltpu.SemaphoreType.DMA],
    )
    return pl.pallas_call(
        _moe_kernel,
        out_shape=jax.ShapeDtypeStruct((n_tiles * tm, d), F32),
        grid_spec=grid_spec,
        compiler_params=_params("arbitrary", "arbitrary"),
        name="moe_experts",
    )(tile_expert, n_used, tok_tiles, x, wg, wl, bg, bl, wd, bd)


def _combine_kernel(pos_hbm, ys_hbm, gates_ref, x_ref, g_ref, b_ref, y_ref,
                    pos_smem, buf, sem_idx, sem_rows):
    i = pl.program_id(0)
    k, tc, _ = buf.shape
    _gather_rows(pos_hbm.at[i], pos_smem, ys_hbm, k * tc, sem_idx, sem_rows,
                 lambda a: buf.at[a % k, pl.ds(a // k, 1)])
    for kk in range(k):
        pltpu.make_async_copy(ys_hbm.at[pl.ds(0, tc)], buf.at[kk], sem_rows).wait()
    gates = gates_ref[...]
    h = sum(gates[:, kk:kk + 1] * buf[kk] for kk in range(k))
    y_ref[...] = _layer_norm(ALPHA * x_ref[...] + h, g_ref[...], b_ref[...])


def _combine_ln(pos_tiles, ys, gates, x, g, b):
    t, d = x.shape
    tc = COMBINE_TILE
    row = lambda width: pl.BlockSpec((tc, width), lambda i: (i, 0))
    full = lambda a: pl.BlockSpec(a.shape, lambda i: (0,) * a.ndim)
    anyspec = pl.BlockSpec(memory_space=pl.ANY)
    return pl.pallas_call(
        _combine_kernel,
        out_shape=jax.ShapeDtypeStruct((t, d), F32),
        grid=(t // tc,),
        in_specs=[anyspec, anyspec, row(LANES), row(d), full(g), full(b)],
        out_specs=row(d),
        scratch_shapes=[pltpu.SMEM((MOE_TOP_K * tc,), jnp.int32),
                        pltpu.VMEM((MOE_TOP_K, tc, d), F32),
                        pltpu.SemaphoreType.DMA,
                        pltpu.SemaphoreType.DMA],
        compiler_params=_params("arbitrary"),
        name="moe_combine_layernorm",
    )(pos_tiles, ys, gates, x, g, b)


def _rope_tables(seq, dim, slab):
    half = dim // 2
    pos = jnp.arange(seq, dtype=F32)
    inv = 1.0 / (ROPE_THETA ** (jnp.arange(0, dim, 2, dtype=F32) / dim))
    ang = pos[:, None] * inv[None, :]
    pad = jnp.zeros((seq, slab // 2 - half), F32)
    cos = jnp.concatenate([jnp.cos(ang), pad, jnp.cos(ang), pad], axis=1)
    sin = jnp.concatenate([-jnp.sin(ang), pad, jnp.sin(ang), pad], axis=1)
    return cos, sin


def _pad_rope_cols(w):
    half = MLA_ROPE_DIM // 2
    z = jnp.zeros(w.shape[:-1] + (LANES // 2 - half,), w.dtype)
    return jnp.concatenate([w[..., :half], z, w[..., half:], z], axis=-1)


def _route(top_e, tm):
    t = top_e.shape[0]
    flat_e = top_e.reshape(-1)
    n_assign = t * MOE_TOP_K
    order = jnp.argsort(flat_e)
    sorted_e = flat_e[order]
    counts = jnp.bincount(flat_e, length=N_EXPERTS)
    padded = (counts + tm - 1) // tm * tm
    start = jnp.cumsum(counts) - counts
    pad_end = jnp.cumsum(padded)
    pad_start = pad_end - padded
    dest = (pad_start[sorted_e] + jnp.arange(n_assign) - start[sorted_e]).astype(jnp.int32)
    n_tiles = n_assign // tm + N_EXPERTS
    row_token = jnp.zeros((n_tiles * tm,), jnp.int32).at[dest].set(
        (order // MOE_TOP_K).astype(jnp.int32))
    pos = jnp.zeros((n_assign,), jnp.int32).at[order].set(dest)
    tile_expert = jnp.minimum(
        jnp.searchsorted(pad_end, jnp.arange(n_tiles) * tm, side='right'),
        N_EXPERTS - 1).astype(jnp.int32)
    n_used = (pad_end[-1:] // tm).astype(jnp.int32)
    return row_token.reshape(n_tiles, tm), pos, tile_expert, n_used


def _moe_layer(x, top_e, gates, w_gu, b_gu, w_down, b_down):
    tok_tiles, pos, tile_expert, n_used = _route(top_e, MOE_TILE_M)
    wg = w_gu[..., 0::2].astype(BF16)
    wl = w_gu[..., 1::2].astype(BF16)
    bg = b_gu[:, None, 0::2]
    bl = b_gu[:, None, 1::2]
    ys = _moe_experts(x, tok_tiles, tile_expert, n_used, wg, wl, bg, bl,
                      w_down.astype(BF16), b_down[:, None, :])
    pos_tiles = pos.reshape(-1, COMBINE_TILE * MOE_TOP_K)
    return ys, pos_tiles


def _router_params(router_w, router_b):
    rw = jnp.pad(router_w, ((0, 0), (0, LANES - N_EXPERTS)))
    rb = jnp.pad(router_b, (0, LANES - N_EXPERTS), constant_values=MASK_VALUE)
    return rw, rb[None, :]


def kernel(x, ln_mix_g, ln_mix_b, ln_ffn_g, ln_ffn_b, w_in_ab, w_out_ab,
           mla_w_dq, mla_q_norm, mla_w_uq, mla_w_dkv, mla_kv_norm, mla_w_ukv, mla_w_out,
           router_w, router_b, moe_w_gu, moe_b_gu, moe_w_down, moe_b_down):
    b, s, d = x.shape
    t = b * s
    xt = x.reshape(t, d)
    cos_h, sin_h = _rope_tables(s, HEAD_DIM, LANES)
    cos_r, sin_r = _rope_tables(s, MLA_ROPE_DIM, LANES)
    row2 = lambda v: v[None, :]

    for layer in range(DEPTH):
        i = layer // 2
        rw, rb = _router_params(router_w[layer], router_b[layer])
        if layer % 2 == 0:
            proj = _matmul(xt, w_in_ab[i].astype(BF16)).reshape(b, s, -1)
            o_a = _longnet(proj, cos_h, sin_h).reshape(t, -1)
            o_b = _moba(proj, cos_h, sin_h).reshape(t, -1)
            w_out = w_out_ab[i].astype(BF16)
            split = A_HEADS * HEAD_DIM
            outs, ws = [o_a, o_b], [w_out[:split], w_out[split:]]
        else:
            w_cat = jnp.concatenate(
                [mla_w_dq[i], mla_w_dkv[i][:, :MLA_KV_RANK],
                 _pad_rope_cols(mla_w_dkv[i][:, MLA_KV_RANK:])], axis=1).astype(BF16)
            cq, ckv, kr = _mla_down(xt, w_cat, row2(mla_q_norm[i]), row2(mla_kv_norm[i]))
            wq = mla_w_uq[i].reshape(MLA_Q_RANK, MLA_HEADS, MLA_NOPE_DIM + MLA_ROPE_DIM)
            wq = jnp.concatenate([wq[..., :MLA_NOPE_DIM],
                                  _pad_rope_cols(wq[..., MLA_NOPE_DIM:])], axis=-1)
            wq = wq.reshape(MLA_Q_RANK, MLA_HEADS * 2 * LANES).astype(BF16)
            o = _mla_attn(cq.reshape(b, s, -1), ckv.reshape(b, s, -1), kr.reshape(b, s, -1),
                          wq, mla_w_ukv[i].astype(BF16), cos_r, sin_r)
            outs, ws = [o.reshape(t, -1)], [mla_w_out[i].astype(BF16)]
        xt, top_e, gates = _outproj_ln(outs, ws, xt, row2(ln_mix_g[layer]),
                                       row2(ln_mix_b[layer]), rw, rb)
        ys, pos_tiles = _moe_layer(xt, top_e[:, :MOE_TOP_K], gates,
                                   moe_w_gu[layer], moe_b_gu[layer],
                                   moe_w_down[layer], moe_b_down[layer])
        xt = _combine_ln(pos_tiles, ys, gates, xt,
                         row2(ln_ffn_g[layer]), row2(ln_ffn_b[layer]))
    return xt.reshape(b, s, d)
```

```python
import functools

import jax
import jax.numpy as jnp
from jax import lax
from jax.experimental import pallas as pl
from jax.experimental.pallas import tpu as pltpu

F32 = jnp.float32
BF16 = jnp.bfloat16

DEPTH = 2
HEAD_DIM = 128
A_HEADS = 12
B_HEADS = 4
DILATED_CONFIGS = ((128, 1), (512, 4), (2048, 16))
MOBA_BLOCK = 256
MOBA_TOP_K = 3
ROPE_THETA = 10000.0
MLA_HEADS = 16
MLA_Q_RANK = 512
MLA_KV_RANK = 512
MLA_NOPE_DIM = 128
MLA_ROPE_DIM = 64
MLA_V_DIM = 128
N_EXPERTS = 32
MOE_TOP_K = 4
SWIGLU_LIMIT = 7.0
SWIGLU_ALPHA = 1.702
LN_EPS = 1e-5
RMS_EPS = 1e-6
ALPHA = (2.0 * DEPTH) ** 0.25

LANES = 128
MASK_VALUE = -1e30
VMEM_LIMIT = 56 * 1024 * 1024

MM_TILE_M = 512
MM_TILE_N = 1024
LN_TILE_M = 256
ATT_TILE_Q = 256
MOE_TILE_M = 1024
MOE_SUB_M = 512
MOE_TILE_F = 256
COMBINE_TILE = 128
DMA_UNROLL = 8


def _params(*semantics):
    return pltpu.CompilerParams(dimension_semantics=semantics,
                                vmem_limit_bytes=VMEM_LIMIT)


def _dot(a, b):
    return jnp.dot(a, b, preferred_element_type=F32)


def _dot_nt(a, b):
    return lax.dot_general(a, b, (((1,), (1,)), ((), ())),
                           preferred_element_type=F32)


def _dot_nt_f32(a, b):
    return lax.dot_general(a, b, (((1,), (1,)), ((), ())),
                           precision=lax.Precision.HIGHEST,
                           preferred_element_type=F32)


def _rope(x, cos, sin):
    return x * cos + pltpu.roll(x, LANES // 2, 1) * sin


def _mm_kernel(x_ref, w_ref, o_ref, xb_ref):
    @pl.when(pl.program_id(1) == 0)
    def _():
        xb_ref[...] = x_ref[...].astype(BF16)
    o_ref[...] = _dot(xb_ref[...], w_ref[...])


def _matmul(x, w):
    m, k = x.shape
    n = w.shape[1]
    tm, tn = min(MM_TILE_M, m), min(MM_TILE_N, n)
    return pl.pallas_call(
        _mm_kernel,
        out_shape=jax.ShapeDtypeStruct((m, n), F32),
        grid=(m // tm, n // tn),
        in_specs=[pl.BlockSpec((tm, k), lambda i, j: (i, 0)),
                  pl.BlockSpec((k, tn), lambda i, j: (0, j))],
        out_specs=pl.BlockSpec((tm, tn), lambda i, j: (i, j)),
        scratch_shapes=[pltpu.VMEM((tm, k), BF16)],
        compiler_params=_params("parallel", "arbitrary"),
        name="proj_matmul",
    )(x, w)


def _longnet_kernel(q_ref, k_ref, v_ref, cos_ref, sin_ref, o_ref,
                    qs, ks, vs, oc, lc, *, seq):
    band = DILATED_CONFIGS[0][0] // DILATED_CONFIGS[0][1]
    cos, sin = cos_ref[...], sin_ref[...]
    qs[...] = _rope(q_ref[0], cos, sin) * (HEAD_DIM ** -0.5)
    ks[...] = _rope(k_ref[0], cos, sin)
    vs[...] = v_ref[0]
    row = lax.broadcasted_iota(jnp.int32, (band, band), 0)
    col = lax.broadcasted_iota(jnp.int32, (band, band), 1)
    own_mask = col <= row
    prev_mask = col >= row

    for c, (window, dil) in enumerate(DILATED_CONFIGS):
        assert window // dil == band and seq % (dil * band) == 0
        nb = seq // (dil * band)
        for r in range(dil):
            for n in range(nb):
                def rows(start):
                    if dil == 1:
                        return pl.ds(start, band)
                    return pl.ds(start, band, stride=dil)
                cur = rows(r + dil * band * n)
                qb = qs[cur, :].astype(BF16)
                kb = ks[cur, :].astype(BF16)
                vb = vs[cur, :].astype(BF16)
                s_own = jnp.where(own_mask, _dot_nt(qb, kb), MASK_VALUE)
                m = jnp.max(s_own, axis=-1, keepdims=True)
                if n > 0:
                    prev = rows(r + dil * band * (n - 1))
                    kp = ks[prev, :].astype(BF16)
                    vp = vs[prev, :].astype(BF16)
                    s_prev = jnp.where(prev_mask, _dot_nt(qb, kp), MASK_VALUE)
                    m = jnp.maximum(m, jnp.max(s_prev, axis=-1, keepdims=True))
                p_own = jnp.exp(s_own - m)
                l = jnp.sum(p_own, axis=-1, keepdims=True)
                acc = _dot(p_own.astype(BF16), vb)
                if n > 0:
                    p_prev = jnp.exp(s_prev - m)
                    l = l + jnp.sum(p_prev, axis=-1, keepdims=True)
                    acc = acc + _dot(p_prev.astype(BF16), vp)
                oc[c, cur, :] = acc / l
                lc[c, cur, :] = jnp.broadcast_to(m + jnp.log(l), (band, LANES))

    lses = [lc[c] for c in range(len(DILATED_CONFIGS))]
    top = functools.reduce(jnp.maximum, lses)
    es = [jnp.exp(v - top) for v in lses]
    num = sum(e * oc[c] for c, e in enumerate(es))
    o_ref[0] = num / sum(es)


def _longnet(proj, cos, sin):
    b, s, _ = proj.shape
    n_cfg = len(DILATED_CONFIGS)
    blk = lambda off: pl.BlockSpec((1, s, HEAD_DIM), lambda i, h: (i, 0, off + h))
    tab = pl.BlockSpec((s, HEAD_DIM), lambda i, h: (0, 0))
    return pl.pallas_call(
        functools.partial(_longnet_kernel, seq=s),
        out_shape=jax.ShapeDtypeStruct((b, s, A_HEADS * HEAD_DIM), F32),
        grid=(b, A_HEADS),
        in_specs=[blk(0), blk(A_HEADS), blk(2 * A_HEADS), tab, tab],
        out_specs=pl.BlockSpec((1, s, HEAD_DIM), lambda i, h: (i, 0, h)),
        scratch_shapes=[pltpu.VMEM((s, HEAD_DIM), F32),
                        pltpu.VMEM((s, HEAD_DIM), F32),
                        pltpu.VMEM((s, HEAD_DIM), F32),
                        pltpu.VMEM((n_cfg, s, HEAD_DIM), F32),
                        pltpu.VMEM((n_cfg, s, LANES), F32)],
        compiler_params=_params("parallel", "parallel"),
        name="longnet_attention",
    )(proj, proj, proj, cos, sin)


def _softmax_blocks(qb, k_ref, v_ref, blocks, tile):
    scores = []
    m = None
    for start, mask in blocks:
        s = _dot_nt(qb, k_ref[pl.ds(start, tile), :])
        if mask is not None:
            s = jnp.where(mask, s, MASK_VALUE)
        scores.append(s)
        bm = jnp.max(s, axis=-1, keepdims=True)
        m = bm if m is None else jnp.maximum(m, bm)
    l = None
    acc = None
    for (start, _), s in zip(blocks, scores):
        p = jnp.exp(s - m)
        pl_sum = jnp.sum(p, axis=-1, keepdims=True)
        pv = _dot(p.astype(BF16), v_ref[pl.ds(start, tile), :])
        l = pl_sum if l is None else l + pl_sum
        acc = pv if acc is None else acc + pv
    return acc / l


def _moba_kernel(q_ref, k_ref, v_ref, cos_ref, sin_ref, o_ref, qs, ks, vs, *, seq):
    blk = MOBA_BLOCK
    nblk = seq // blk
    cos, sin = cos_ref[...], sin_ref[...]
    q = _rope(q_ref[0], cos, sin)
    k = _rope(k_ref[0], cos, sin)
    qs[...] = (q * (HEAD_DIM ** -0.5)).astype(BF16)
    ks[...] = k.astype(BF16)
    vs[...] = v_ref[0].astype(BF16)
    k_mean = jnp.concatenate(
        [jnp.mean(k[n * blk:(n + 1) * blk], axis=0, keepdims=True) for n in range(nblk)],
        axis=0)
    gate = _dot_nt_f32(q, k_mean)
    row = lax.broadcasted_iota(jnp.int32, (blk, blk), 0)
    col = lax.broadcasted_iota(jnp.int32, (blk, blk), 1)
    causal = col <= row
    n_top = min(MOBA_TOP_K, nblk)

    for i in range(nblk):
        g = gate[i * blk:(i + 1) * blk, :]
        blocks = [(i * blk, causal)]
        for n in range(i):
            if i <= n_top:
                blocks.append((n * blk, None))
                continue
            gn = g[:, n:n + 1]
            rank = jnp.zeros((blk, 1), jnp.int32)
            for o in range(i):
                if o == n:
                    continue
                go = g[:, o:o + 1]
                ahead = (go > gn) | (go == gn) if o < n else (go > gn)
                rank = rank + ahead.astype(jnp.int32)
            blocks.append((n * blk, rank < n_top))
        o_ref[0, pl.ds(i * blk, blk), :] = _softmax_blocks(
            qs[pl.ds(i * blk, blk), :], ks, vs, blocks, blk)


def _moba(proj, cos, sin):
    b, s, _ = proj.shape
    base = 3 * A_HEADS
    blk = lambda off: pl.BlockSpec((1, s, HEAD_DIM), lambda i, h: (i, 0, base + off + h))
    tab = pl.BlockSpec((s, HEAD_DIM), lambda i, h: (0, 0))
    return pl.pallas_call(
        functools.partial(_moba_kernel, seq=s),
        out_shape=jax.ShapeDtypeStruct((b, s, B_HEADS * HEAD_DIM), F32),
        grid=(b, B_HEADS),
        in_specs=[blk(0), blk(B_HEADS), blk(2 * B_HEADS), tab, tab],
        out_specs=pl.BlockSpec((1, s, HEAD_DIM), lambda i, h: (i, 0, h)),
        scratch_shapes=[pltpu.VMEM((s, HEAD_DIM), BF16)] * 3,
        compiler_params=_params("parallel", "parallel"),
        name="moba_attention",
    )(proj, proj, proj, cos, sin)


def _layer_norm(z, g, b):
    mu = jnp.mean(z, axis=-1, keepdims=True)
    zc = z - mu
    var = jnp.mean(zc * zc, axis=-1, keepdims=True)
    return zc * lax.rsqrt(var + LN_EPS) * g + b


def _route_topk(y, rw_ref, rb_ref, e_ref, g_ref):
    logits = jnp.dot(y, rw_ref[...], precision=lax.Precision.HIGHEST,
                     preferred_element_type=F32) + rb_ref[...]
    lane = lax.broadcasted_iota(jnp.int32, logits.shape, 1)
    vals, idxs = [], []
    for _ in range(MOE_TOP_K):
        v = jnp.max(logits, axis=-1, keepdims=True)
        ix = jnp.min(jnp.where(logits == v, lane, LANES), axis=-1, keepdims=True)
        vals.append(v)
        idxs.append(ix)
        logits = jnp.where(lane == ix, -jnp.inf, logits)
    exps = [jnp.exp(v - vals[0]) for v in vals]
    denom = sum(exps)
    e_out = jnp.zeros(lane.shape, jnp.int32)
    g_out = jnp.zeros(lane.shape, F32)
    for kk in range(MOE_TOP_K):
        e_out = jnp.where(lane == kk, idxs[kk], e_out)
        g_out = jnp.where(lane == kk, exps[kk] / denom, g_out)
    e_ref[...] = e_out
    g_ref[...] = g_out


_HI16 = 0xFFFF0000


def _pack_bf16_halves(y):
    half = y.shape[1] // 2
    bits = lambda v: lax.bitcast_convert_type(v.astype(BF16).astype(F32), jnp.uint32)
    return (bits(y[:, :half]) >> 16) | (bits(y[:, half:]) & jnp.uint32(_HI16))


def _unpack_bf16_halves(pk):
    lo = lax.bitcast_convert_type(pk << 16, F32).astype(BF16)
    hi = lax.bitcast_convert_type(pk & jnp.uint32(_HI16), F32).astype(BF16)
    return lo, hi


def _outproj_ln_kernel(*refs, n_in):
    o_refs, w_refs = refs[:n_in], refs[n_in:2 * n_in]
    x_ref, g_ref, b_ref, rw_ref, rb_ref, y_ref, pk_ref, e_ref, gt_ref = refs[2 * n_in:]
    h = sum(_dot(o[...].astype(BF16), w[...]) for o, w in zip(o_refs, w_refs))
    y = _layer_norm(ALPHA * x_ref[...] + h, g_ref[...], b_ref[...])
    y_ref[...] = y
    pk_ref[...] = _pack_bf16_halves(y)
    _route_topk(y, rw_ref, rb_ref, e_ref, gt_ref)


def _outproj_ln(outs, ws, x, g, b, rw, rb):
    t, d = x.shape
    tm = LN_TILE_M
    n_in = len(outs)
    row = lambda width: pl.BlockSpec((tm, width), lambda i: (i, 0))
    full = lambda a: pl.BlockSpec(a.shape, lambda i: (0,) * a.ndim)
    return pl.pallas_call(
        functools.partial(_outproj_ln_kernel, n_in=n_in),
        out_shape=(jax.ShapeDtypeStruct((t, d), F32),
                   jax.ShapeDtypeStruct((t, d // 2), jnp.uint32),
                   jax.ShapeDtypeStruct((t, LANES), jnp.int32),
                   jax.ShapeDtypeStruct((t, LANES), F32)),
        grid=(t // tm,),
        in_specs=([row(o.shape[1]) for o in outs] + [full(w) for w in ws]
                  + [row(d), full(g), full(b), full(rw), full(rb)]),
        out_specs=(row(d), row(d // 2), row(LANES), row(LANES)),
        compiler_params=_params("parallel"),
        name="outproj_layernorm_router",
    )(*outs, *ws, x, g, b, rw, rb)


def _rms_norm(x, g):
    return x * lax.rsqrt(jnp.mean(x * x, axis=-1, keepdims=True) + RMS_EPS) * g


def _mla_down_kernel(x_ref, w_ref, qn_ref, kvn_ref, cq_ref, ckv_ref, kr_ref):
    a = _dot(x_ref[...].astype(BF16), w_ref[...])
    cq_ref[...] = _rms_norm(a[:, :MLA_Q_RANK], qn_ref[...]).astype(BF16)
    ckv_ref[...] = _rms_norm(a[:, MLA_Q_RANK:MLA_Q_RANK + MLA_KV_RANK],
                             kvn_ref[...]).astype(BF16)
    kr_ref[...] = a[:, MLA_Q_RANK + MLA_KV_RANK:]


def _mla_down(x, w_cat, q_norm, kv_norm):
    t, d = x.shape
    tm = MM_TILE_M
    row = lambda width: pl.BlockSpec((tm, width), lambda i: (i, 0))
    full = lambda a: pl.BlockSpec(a.shape, lambda i: (0,) * a.ndim)
    return pl.pallas_call(
        _mla_down_kernel,
        out_shape=(jax.ShapeDtypeStruct((t, MLA_Q_RANK), BF16),
                   jax.ShapeDtypeStruct((t, MLA_KV_RANK), BF16),
                   jax.ShapeDtypeStruct((t, LANES), F32)),
        grid=(t // tm,),
        in_specs=[row(d), full(w_cat), full(q_norm), full(kv_norm)],
        out_specs=(row(MLA_Q_RANK), row(MLA_KV_RANK), row(LANES)),
        compiler_params=_params("parallel"),
        name="mla_down_rmsnorm",
    )(x, w_cat, q_norm, kv_norm)


def _mla_attn_kernel(cq_ref, ckv_ref, kr_ref, wq_ref, wkv_ref, cos_ref, sin_ref,
                     o_ref, qs, ks, vs, *, seq):
    tile = ATT_TILE_Q
    scale = (MLA_NOPE_DIM + MLA_ROPE_DIM) ** -0.5
    cos, sin = cos_ref[...], sin_ref[...]
    q = _dot(cq_ref[0], wq_ref[...])
    kv = _dot(ckv_ref[0], wkv_ref[...])
    qs[:, :LANES] = (q[:, :LANES] * scale).astype(BF16)
    qs[:, LANES:] = (_rope(q[:, LANES:], cos, sin) * scale).astype(BF16)
    ks[:, :LANES] = kv[:, :MLA_NOPE_DIM].astype(BF16)
    ks[:, LANES:] = _rope(kr_ref[0], cos, sin).astype(BF16)
    vs[...] = kv[:, MLA_NOPE_DIM:].astype(BF16)
    row = lax.broadcasted_iota(jnp.int32, (tile, tile), 0)
    col = lax.broadcasted_iota(jnp.int32, (tile, tile), 1)
    causal = col <= row
    for i in range(seq // tile):
        blocks = [(i * tile, causal)] + [(n * tile, None) for n in range(i)]
        o_ref[0, pl.ds(i * tile, tile), :] = _softmax_blocks(
            qs[pl.ds(i * tile, tile), :], ks, vs, blocks, tile)


def _mla_attn(cq, ckv, kr, wq, wkv, cos, sin):
    b, s, _ = cq.shape
    per_b = lambda width: pl.BlockSpec((1, s, width), lambda i, h: (i, 0, 0))
    head_w = lambda a: pl.BlockSpec((a.shape[0], 2 * LANES), lambda i, h: (0, h))
    tab = pl.BlockSpec((s, LANES), lambda i, h: (0, 0))
    return pl.pallas_call(
        functools.partial(_mla_attn_kernel, seq=s),
        out_shape=jax.ShapeDtypeStruct((b, s, MLA_HEADS * MLA_V_DIM), F32),
        grid=(b, MLA_HEADS),
        in_specs=[per_b(MLA_Q_RANK), per_b(MLA_KV_RANK), per_b(LANES),
                  head_w(wq), head_w(wkv), tab, tab],
        out_specs=pl.BlockSpec((1, s, MLA_V_DIM), lambda i, h: (i, 0, h)),
        scratch_shapes=[pltpu.VMEM((s, 2 * LANES), BF16),
                        pltpu.VMEM((s, 2 * LANES), BF16),
                        pltpu.VMEM((s, MLA_V_DIM), BF16)],
        compiler_params=_params("parallel", "arbitrary"),
        name="mla_attention",
    )(cq, ckv, kr, wq, wkv, cos, sin)


def _start_row_gather(idx_hbm_row, idx_smem, src_hbm, n, sem_idx, sem_rows, dst_of):
    cp = pltpu.make_async_copy(idx_hbm_row, idx_smem, sem_idx)
    cp.start()
    cp.wait()

    def issue(c, carry):
        for u in range(DMA_UNROLL):
            a = c * DMA_UNROLL + u
            pltpu.make_async_copy(src_hbm.at[pl.ds(idx_smem[a], 1)], dst_of(a),
                                  sem_rows).start()
        return carry
    lax.fori_loop(0, n // DMA_UNROLL, issue, 0)


def _moe_kernel(te_ref, nu_ref, ns_ref, tok_hbm, x_hbm, wgu_ref, bgu_ref, wd_ref,
                bd_ref, cmp_ref, o_ref, tok_smem, xbuf, xb, wgu_b, wd_b,
                sem_idx, sem_rows):
    t, j = pl.program_id(0), pl.program_id(1)
    _, tm, half = xbuf.shape
    sub = MOE_SUB_M
    two_f = wgu_b.shape[1]
    used = t < nu_ref[0]
    slot = t % 2

    def start_gather(tile, sl):
        _start_row_gather(tok_hbm.at[tile], tok_smem, x_hbm, ns_ref[tile] * sub,
                          sem_idx, sem_rows.at[sl],
                          lambda a: xbuf.at[sl, pl.ds(a, 1)])

    def for_valid_subs(fn, otherwise=None):
        for s in range(tm // sub):
            pl.when(s < ns_ref[t])(functools.partial(fn, s))
            if otherwise is not None:
                pl.when(s >= ns_ref[t])(functools.partial(otherwise, s))

    @pl.when(jnp.logical_and(used, j == 0))
    def _():
        @pl.when(t == 0)
        def _():
            start_gather(0, 0)

        def wait_rows(s):
            pltpu.make_async_copy(x_hbm.at[pl.ds(0, sub)],
                                  xbuf.at[slot, pl.ds(s * sub, sub)],
                                  sem_rows.at[slot]).wait()

        for_valid_subs(wait_rows)

        def land(s):
            rows = pl.ds(s * sub, sub)
            lo, hi = _unpack_bf16_halves(xbuf[slot, rows, :])
            xb[rows, :half] = lo
            xb[rows, half:] = hi
            o_ref[rows, :] = jnp.broadcast_to(bd_ref[0, 0], (sub, o_ref.shape[1]))

        def blank(s):
            o_ref[pl.ds(s * sub, sub), :] = jnp.zeros((sub, o_ref.shape[1]), F32)

        for_valid_subs(land, blank)

        @pl.when(t + 1 < nu_ref[0])
        def _():
            start_gather(t + 1, 1 - slot)

    @pl.when(jnp.logical_and(jnp.logical_not(used), j == 0))
    def _():
        o_ref[...] = jnp.zeros(o_ref.shape, F32)

    @pl.when(used)
    def _():
        wgu_b[...] = wgu_ref[0, 0].astype(BF16)
        wd_b[...] = wd_ref[0, 0].astype(BF16)
        lane = lax.broadcasted_iota(jnp.int32, (sub, two_f), 1)
        is_glu = (lane & 1) == 0

        def compute(s):
            rows = pl.ds(s * sub, sub)
            h = _dot(xb[rows, :], wgu_b[...]) + bgu_ref[0, 0]
            glu = jnp.minimum(h, SWIGLU_LIMIT)
            lin = jnp.clip(h, -SWIGLU_LIMIT, SWIGLU_LIMIT) + 1.0
            lin = jnp.concatenate(
                [pltpu.roll(lin[:, c:c + LANES], LANES - 1, 1)
                 for c in range(0, two_f, LANES)], axis=1)
            act = jnp.where(is_glu, glu * jax.nn.sigmoid(SWIGLU_ALPHA * glu) * lin, 0.0)
            act = _dot(act.astype(BF16), cmp_ref[...])
            o_ref[rows, :] += _dot(act.astype(BF16), wd_b[...])

        for_valid_subs(compute)


def _moe_experts(x_packed, tok_tiles, tile_expert, n_used, n_sub, w_gu, b_gu, w_down,
                 b_down, layer):
    n_tiles, tm = tok_tiles.shape
    half = x_packed.shape[1]
    d = 2 * half
    ff = w_down.shape[2]
    tf = MOE_TILE_F
    n_f = ff // tf
    compact = (jnp.arange(2 * tf)[:, None] == 2 * jnp.arange(tf)[None, :]).astype(BF16)

    def tile(i, nu):
        return jnp.minimum(i, nu[0] - 1)

    def chunk(i, j, nu):
        return jnp.where(i < nu[0], j, n_f - 1)

    grid_spec = pltpu.PrefetchScalarGridSpec(
        num_scalar_prefetch=3,
        grid=(n_tiles, n_f),
        in_specs=[
            pl.BlockSpec(memory_space=pl.ANY),
            pl.BlockSpec(memory_space=pl.ANY),
            pl.BlockSpec((1, 1, d, 2 * tf),
                         lambda i, j, te, nu, ns: (layer, te[tile(i, nu)], 0, chunk(i, j, nu))),
            pl.BlockSpec((1, 1, 1, 2 * tf),
                         lambda i, j, te, nu, ns: (layer, te[tile(i, nu)], 0, chunk(i, j, nu))),
            pl.BlockSpec((1, 1, tf, d),
                         lambda i, j, te, nu, ns: (layer, te[tile(i, nu)], chunk(i, j, nu), 0)),
            pl.BlockSpec((1, 1, 1, d),
                         lambda i, j, te, nu, ns: (layer, te[tile(i, nu)], 0, 0)),
            pl.BlockSpec((2 * tf, tf), lambda i, j, te, nu, ns: (0, 0)),
        ],
        out_specs=pl.BlockSpec((tm, d), lambda i, j, te, nu, ns: (i, 0)),
        scratch_shapes=[pltpu.SMEM((tm,), jnp.int32),
                        pltpu.VMEM((2, tm, half), jnp.uint32),
                        pltpu.VMEM((tm, d), BF16),
                        pltpu.VMEM((d, 2 * tf), BF16),
                        pltpu.VMEM((tf, d), BF16),
                        pltpu.SemaphoreType.DMA,
                        pltpu.SemaphoreType.DMA((2,))],
    )
    return pl.pallas_call(
        _moe_kernel,
        out_shape=jax.ShapeDtypeStruct((n_tiles * tm, d), F32),
        grid_spec=grid_spec,
        compiler_params=_params("arbitrary", "arbitrary"),
        name="moe_experts",
    )(tile_expert, n_used, n_sub, tok_tiles, x_packed, w_gu,
      b_gu[:, :, None, :], w_down, b_down[:, :, None, :], compact)


def _combine_kernel(pos_hbm, ys_hbm, gates_ref, x_ref, g_ref, b_ref, y_ref,
                    pos_smem, buf, sem_idx, sem_rows):
    i = pl.program_id(0)
    _, k, tc, _ = buf.shape
    slot = i % 2

    def start_gather(tile, sl):
        _start_row_gather(pos_hbm.at[tile], pos_smem, ys_hbm, k * tc, sem_idx,
                          sem_rows.at[sl],
                          lambda a: buf.at[sl, a % k, pl.ds(a // k, 1)])

    @pl.when(i == 0)
    def _():
        start_gather(0, 0)

    for kk in range(k):
        pltpu.make_async_copy(ys_hbm.at[pl.ds(0, tc)], buf.at[slot, kk],
                              sem_rows.at[slot]).wait()

    @pl.when(i + 1 < pl.num_programs(0))
    def _():
        start_gather(i + 1, 1 - slot)

    gates = gates_ref[...]
    h = sum(gates[:, kk:kk + 1] * buf[slot, kk] for kk in range(k))
    y_ref[...] = _layer_norm(ALPHA * x_ref[...] + h, g_ref[...], b_ref[...])


def _combine_ln(pos_tiles, ys, gates, x, g, b):
    t, d = x.shape
    tc = COMBINE_TILE
    row = lambda width: pl.BlockSpec((tc, width), lambda i: (i, 0))
    full = lambda a: pl.BlockSpec(a.shape, lambda i: (0,) * a.ndim)
    anyspec = pl.BlockSpec(memory_space=pl.ANY)
    return pl.pallas_call(
        _combine_kernel,
        out_shape=jax.ShapeDtypeStruct((t, d), F32),
        grid=(t // tc,),
        in_specs=[anyspec, anyspec, row(LANES), row(d), full(g), full(b)],
        out_specs=row(d),
        scratch_shapes=[pltpu.SMEM((MOE_TOP_K * tc,), jnp.int32),
                        pltpu.VMEM((2, MOE_TOP_K, tc, d), F32),
                        pltpu.SemaphoreType.DMA,
                        pltpu.SemaphoreType.DMA((2,))],
        compiler_params=_params("arbitrary"),
        name="moe_combine_layernorm",
    )(pos_tiles, ys, gates, x, g, b)


def _rope_tables(seq, dim, slab):
    half = dim // 2
    pos = jnp.arange(seq, dtype=F32)
    inv = 1.0 / (ROPE_THETA ** (jnp.arange(0, dim, 2, dtype=F32) / dim))
    ang = pos[:, None] * inv[None, :]
    pad = jnp.zeros((seq, slab // 2 - half), F32)
    cos = jnp.concatenate([jnp.cos(ang), pad, jnp.cos(ang), pad], axis=1)
    sin = jnp.concatenate([-jnp.sin(ang), pad, jnp.sin(ang), pad], axis=1)
    return cos, sin


def _pad_rope_cols(w):
    half = MLA_ROPE_DIM // 2
    z = jnp.zeros(w.shape[:-1] + (LANES // 2 - half,), w.dtype)
    return jnp.concatenate([w[..., :half], z, w[..., half:], z], axis=-1)


def _route(top_e, tm, sub):
    t = top_e.shape[0]
    i32 = jnp.int32
    flat_e = top_e.reshape(-1)
    n_assign = t * MOE_TOP_K
    n_tiles = n_assign // tm + N_EXPERTS
    order = jnp.argsort(flat_e).astype(i32)
    sorted_e = flat_e[order]
    counts = jnp.sum(flat_e[:, None] == jnp.arange(N_EXPERTS, dtype=i32)[None, :],
                     axis=0, dtype=i32)
    padded = (counts + tm - 1) // tm * tm
    start = jnp.cumsum(counts) - counts
    pad_end = jnp.cumsum(padded)
    pad_start = pad_end - padded
    tile_start = jnp.arange(n_tiles, dtype=i32) * tm
    tile_expert = jnp.minimum(
        jnp.sum(pad_end[None, :] <= tile_start[:, None], axis=1, dtype=i32), N_EXPERTS - 1)
    n_used = (pad_end[-1:] // tm).astype(i32)
    first = tile_start - pad_start[tile_expert]
    tile_rows = jnp.clip(counts[tile_expert] - first, 0, tm)
    n_sub = (tile_rows + sub - 1) // sub
    k_in_e = first[:, None] + jnp.arange(tm, dtype=i32)[None, :]
    valid = k_in_e < counts[tile_expert][:, None]
    src = jnp.clip(start[tile_expert][:, None] + k_in_e, 0, n_assign - 1)
    row_token = jnp.where(valid, order[src] // MOE_TOP_K, 0).astype(i32)
    dest = (pad_start[sorted_e] + jnp.arange(n_assign, dtype=i32) - start[sorted_e]).astype(i32)
    _, pos = lax.sort_key_val(order, dest)
    return row_token, pos, tile_expert, n_used, n_sub.astype(i32)


def _moe_layer(x_packed, top_e, w_gu, b_gu, w_down, b_down, layer):
    tok_tiles, pos, tile_expert, n_used, n_sub = _route(top_e, MOE_TILE_M, MOE_SUB_M)
    ys = _moe_experts(x_packed, tok_tiles, tile_expert, n_used, n_sub,
                      w_gu, b_gu, w_down, b_down, layer)
    pos_tiles = pos.reshape(-1, COMBINE_TILE * MOE_TOP_K)
    return ys, pos_tiles


def _router_params(router_w, router_b):
    rw = jnp.pad(router_w, ((0, 0), (0, LANES - N_EXPERTS)))
    rb = jnp.pad(router_b, (0, LANES - N_EXPERTS), constant_values=MASK_VALUE)
    return rw, rb[None, :]


def kernel(x, ln_mix_g, ln_mix_b, ln_ffn_g, ln_ffn_b, w_in_ab, w_out_ab,
           mla_w_dq, mla_q_norm, mla_w_uq, mla_w_dkv, mla_kv_norm, mla_w_ukv, mla_w_out,
           router_w, router_b, moe_w_gu, moe_b_gu, moe_w_down, moe_b_down):
    b, s, d = x.shape
    t = b * s
    xt = x.reshape(t, d)
    cos_h, sin_h = _rope_tables(s, HEAD_DIM, LANES)
    cos_r, sin_r = _rope_tables(s, MLA_ROPE_DIM, LANES)
    row2 = lambda v: v[None, :]

    for layer in range(DEPTH):
        i = layer // 2
        rw, rb = _router_params(router_w[layer], router_b[layer])
        if layer % 2 == 0:
            proj = _matmul(xt, w_in_ab[i].astype(BF16)).reshape(b, s, -1)
            o_a = _longnet(proj, cos_h, sin_h).reshape(t, -1)
            o_b = _moba(proj, cos_h, sin_h).reshape(t, -1)
            w_out = w_out_ab[i].astype(BF16)
            split = A_HEADS * HEAD_DIM
            outs, ws = [o_a, o_b], [w_out[:split], w_out[split:]]
        else:
            w_cat = jnp.concatenate(
                [mla_w_dq[i], mla_w_dkv[i][:, :MLA_KV_RANK],
                 _pad_rope_cols(mla_w_dkv[i][:, MLA_KV_RANK:])], axis=1).astype(BF16)
            cq, ckv, kr = _mla_down(xt, w_cat, row2(mla_q_norm[i]), row2(mla_kv_norm[i]))
            wq = mla_w_uq[i].reshape(MLA_Q_RANK, MLA_HEADS, MLA_NOPE_DIM + MLA_ROPE_DIM)
            wq = jnp.concatenate([wq[..., :MLA_NOPE_DIM],
                                  _pad_rope_cols(wq[..., MLA_NOPE_DIM:])], axis=-1)
            wq = wq.reshape(MLA_Q_RANK, MLA_HEADS * 2 * LANES).astype(BF16)
            o = _mla_attn(cq.reshape(b, s, -1), ckv.reshape(b, s, -1), kr.reshape(b, s, -1),
                          wq, mla_w_ukv[i].astype(BF16), cos_r, sin_r)
            outs, ws = [o.reshape(t, -1)], [mla_w_out[i].astype(BF16)]
        xt, x_packed, top_e, gates = _outproj_ln(outs, ws, xt, row2(ln_mix_g[layer]),
                                                 row2(ln_mix_b[layer]), rw, rb)
        ys, pos_tiles = _moe_layer(x_packed, top_e[:, :MOE_TOP_K], moe_w_gu, moe_b_gu,
                                   moe_w_down, moe_b_down, layer)
        xt = _combine_ln(pos_tiles, ys, gates, xt,
                         row2(ln_ffn_g[layer]), row2(ln_ffn_b[layer]))
    return xt.reshape(b, s, d)
```

```python
import functools

import jax
import jax.numpy as jnp
from jax import lax
from jax.experimental import pallas as pl
from jax.experimental.pallas import tpu as pltpu

F32 = jnp.float32
BF16 = jnp.bfloat16

DEPTH = 2
HEAD_DIM = 128
A_HEADS = 12
B_HEADS = 4
DILATED_CONFIGS = ((128, 1), (512, 4), (2048, 16))
MOBA_BLOCK = 256
MOBA_TOP_K = 3
ROPE_THETA = 10000.0
MLA_HEADS = 16
MLA_Q_RANK = 512
MLA_KV_RANK = 512
MLA_NOPE_DIM = 128
MLA_ROPE_DIM = 64
MLA_V_DIM = 128
N_EXPERTS = 32
MOE_TOP_K = 4
SWIGLU_LIMIT = 7.0
SWIGLU_ALPHA = 1.702
LN_EPS = 1e-5
RMS_EPS = 1e-6
ALPHA = (2.0 * DEPTH) ** 0.25

LANES = 128
MASK_VALUE = -1e30
VMEM_LIMIT = 56 * 1024 * 1024

MM_TILE_M = 512
MM_TILE_N = 1024
LN_TILE_M = 256
ATT_TILE_Q = 256
MOE_TILE_M = 1024
MOE_SUB_M = 512
MOE_TILE_F = 512
MOE_TILE_N = 1024
COMBINE_TILE = 128
DMA_UNROLL = 8


def _params(*semantics):
    return pltpu.CompilerParams(dimension_semantics=semantics,
                                vmem_limit_bytes=VMEM_LIMIT)


def _dot(a, b):
    return jnp.dot(a, b, preferred_element_type=F32)


def _dot_nt(a, b):
    return lax.dot_general(a, b, (((1,), (1,)), ((), ())),
                           preferred_element_type=F32)


def _dot_nt_f32(a, b):
    return lax.dot_general(a, b, (((1,), (1,)), ((), ())),
                           precision=lax.Precision.HIGHEST,
                           preferred_element_type=F32)


def _rope(x, cos, sin):
    return x * cos + pltpu.roll(x, LANES // 2, 1) * sin


def _mm_kernel(x_ref, w_ref, o_ref, xb_ref):
    @pl.when(pl.program_id(1) == 0)
    def _():
        xb_ref[...] = x_ref[...].astype(BF16)
    o_ref[...] = _dot(xb_ref[...], w_ref[...])


def _matmul(x, w):
    m, k = x.shape
    n = w.shape[1]
    tm, tn = min(MM_TILE_M, m), min(MM_TILE_N, n)
    return pl.pallas_call(
        _mm_kernel,
        out_shape=jax.ShapeDtypeStruct((m, n), F32),
        grid=(m // tm, n // tn),
        in_specs=[pl.BlockSpec((tm, k), lambda i, j: (i, 0)),
                  pl.BlockSpec((k, tn), lambda i, j: (0, j))],
        out_specs=pl.BlockSpec((tm, tn), lambda i, j: (i, j)),
        scratch_shapes=[pltpu.VMEM((tm, k), BF16)],
        compiler_params=_params("parallel", "arbitrary"),
        name="proj_matmul",
    )(x, w)


def _longnet_kernel(q_ref, k_ref, v_ref, cos_ref, sin_ref, o_ref,
                    qs, ks, vs, oc, lc, *, seq):
    band = DILATED_CONFIGS[0][0] // DILATED_CONFIGS[0][1]
    cos, sin = cos_ref[...], sin_ref[...]
    qs[...] = _rope(q_ref[0], cos, sin) * (HEAD_DIM ** -0.5)
    ks[...] = _rope(k_ref[0], cos, sin)
    vs[...] = v_ref[0]
    row = lax.broadcasted_iota(jnp.int32, (band, 2 * band), 0)
    col = lax.broadcasted_iota(jnp.int32, (band, 2 * band), 1)
    band_mask = jnp.logical_and(col >= row, col <= row + band)
    own_mask = (lax.broadcasted_iota(jnp.int32, (band, band), 1)
                <= lax.broadcasted_iota(jnp.int32, (band, band), 0))

    for c, (window, dil) in enumerate(DILATED_CONFIGS):
        assert window // dil == band and seq % (dil * band) == 0
        nb = seq // (dil * band)
        for r in range(dil):
            for n in range(nb):
                def rows(start, size):
                    if dil == 1:
                        return pl.ds(start, size)
                    return pl.ds(start, size, stride=dil)
                start = r + dil * band * n
                cur = rows(start, band)
                keys = cur if n == 0 else rows(start - dil * band, 2 * band)
                mask = own_mask if n == 0 else band_mask
                qb = qs[cur, :].astype(BF16)
                kb = ks[keys, :].astype(BF16)
                vb = vs[keys, :].astype(BF16)
                s = jnp.where(mask, _dot_nt(qb, kb), MASK_VALUE)
                m = jnp.max(s, axis=-1, keepdims=True)
                p = jnp.exp(s - m)
                l = jnp.sum(p, axis=-1, keepdims=True)
                acc = _dot(p.astype(BF16), vb)
                oc[c, cur, :] = acc / l
                lc[c, cur, :] = jnp.broadcast_to(m + jnp.log(l), (band, LANES))

    lses = [lc[c] for c in range(len(DILATED_CONFIGS))]
    top = functools.reduce(jnp.maximum, lses)
    es = [jnp.exp(v - top) for v in lses]
    num = sum(e * oc[c] for c, e in enumerate(es))
    o_ref[0] = num / sum(es)


def _longnet(proj, cos, sin):
    b, s, _ = proj.shape
    n_cfg = len(DILATED_CONFIGS)
    blk = lambda off: pl.BlockSpec((1, s, HEAD_DIM), lambda i, h: (i, 0, off + h))
    tab = pl.BlockSpec((s, HEAD_DIM), lambda i, h: (0, 0))
    return pl.pallas_call(
        functools.partial(_longnet_kernel, seq=s),
        out_shape=jax.ShapeDtypeStruct((b, s, A_HEADS * HEAD_DIM), F32),
        grid=(b, A_HEADS),
        in_specs=[blk(0), blk(A_HEADS), blk(2 * A_HEADS), tab, tab],
        out_specs=pl.BlockSpec((1, s, HEAD_DIM), lambda i, h: (i, 0, h)),
        scratch_shapes=[pltpu.VMEM((s, HEAD_DIM), F32),
                        pltpu.VMEM((s, HEAD_DIM), F32),
                        pltpu.VMEM((s, HEAD_DIM), F32),
                        pltpu.VMEM((n_cfg, s, HEAD_DIM), F32),
                        pltpu.VMEM((n_cfg, s, LANES), F32)],
        compiler_params=_params("parallel", "parallel"),
        name="longnet_attention",
    )(proj, proj, proj, cos, sin)


def _softmax_blocks(qb, k_ref, v_ref, blocks, tile):
    scores = []
    m = None
    for start, mask in blocks:
        s = _dot_nt(qb, k_ref[pl.ds(start, tile), :])
        if mask is not None:
            s = jnp.where(mask, s, MASK_VALUE)
        scores.append(s)
        bm = jnp.max(s, axis=-1, keepdims=True)
        m = bm if m is None else jnp.maximum(m, bm)
    l = None
    acc = None
    for (start, _), s in zip(blocks, scores):
        p = jnp.exp(s - m)
        pl_sum = jnp.sum(p, axis=-1, keepdims=True)
        pv = _dot(p.astype(BF16), v_ref[pl.ds(start, tile), :])
        l = pl_sum if l is None else l + pl_sum
        acc = pv if acc is None else acc + pv
    return acc / l


def _moba_kernel(q_ref, k_ref, v_ref, cos_ref, sin_ref, o_ref, qs, ks, vs, *, seq):
    blk = MOBA_BLOCK
    nblk = seq // blk
    cos, sin = cos_ref[...], sin_ref[...]
    q = _rope(q_ref[0], cos, sin)
    k = _rope(k_ref[0], cos, sin)
    qs[...] = (q * (HEAD_DIM ** -0.5)).astype(BF16)
    ks[...] = k.astype(BF16)
    vs[...] = v_ref[0].astype(BF16)
    k_mean = jnp.concatenate(
        [jnp.mean(k[n * blk:(n + 1) * blk], axis=0, keepdims=True) for n in range(nblk)],
        axis=0)
    gate = _dot_nt_f32(q, k_mean)
    row = lax.broadcasted_iota(jnp.int32, (blk, blk), 0)
    col = lax.broadcasted_iota(jnp.int32, (blk, blk), 1)
    causal = col <= row
    n_top = min(MOBA_TOP_K, nblk)

    for i in range(nblk):
        g = gate[i * blk:(i + 1) * blk, :]
        blocks = [(i * blk, causal)]
        for n in range(i):
            if i <= n_top:
                blocks.append((n * blk, None))
                continue
            gn = g[:, n:n + 1]
            rank = jnp.zeros((blk, 1), jnp.int32)
            for o in range(i):
                if o == n:
                    continue
                go = g[:, o:o + 1]
                ahead = (go > gn) | (go == gn) if o < n else (go > gn)
                rank = rank + ahead.astype(jnp.int32)
            blocks.append((n * blk, rank < n_top))
        o_ref[0, pl.ds(i * blk, blk), :] = _softmax_blocks(
            qs[pl.ds(i * blk, blk), :], ks, vs, blocks, blk)


def _moba(proj, cos, sin):
    b, s, _ = proj.shape
    base = 3 * A_HEADS
    blk = lambda off: pl.BlockSpec((1, s, HEAD_DIM), lambda i, h: (i, 0, base + off + h))
    tab = pl.BlockSpec((s, HEAD_DIM), lambda i, h: (0, 0))
    return pl.pallas_call(
        functools.partial(_moba_kernel, seq=s),
        out_shape=jax.ShapeDtypeStruct((b, s, B_HEADS * HEAD_DIM), F32),
        grid=(b, B_HEADS),
        in_specs=[blk(0), blk(B_HEADS), blk(2 * B_HEADS), tab, tab],
        out_specs=pl.BlockSpec((1, s, HEAD_DIM), lambda i, h: (i, 0, h)),
        scratch_shapes=[pltpu.VMEM((s, HEAD_DIM), BF16)] * 3,
        compiler_params=_params("parallel", "parallel"),
        name="moba_attention",
    )(proj, proj, proj, cos, sin)


def _layer_norm(z, g, b):
    mu = jnp.mean(z, axis=-1, keepdims=True)
    zc = z - mu
    var = jnp.mean(zc * zc, axis=-1, keepdims=True)
    return zc * lax.rsqrt(var + LN_EPS) * g + b


def _route_topk(y, rw_ref, rb_ref, e_ref, g_ref):
    y_hi = y.astype(BF16)
    y_lo = (y - y_hi.astype(F32)).astype(BF16)
    hi_both = _dot(y_hi, rw_ref[...])
    lo_hi = _dot(y_lo, rw_ref[:, :LANES])
    logits = hi_both[:, :LANES] + (hi_both[:, LANES:] + lo_hi) + rb_ref[...]
    lane = lax.broadcasted_iota(jnp.int32, logits.shape, 1)
    vals, idxs = [], []
    for _ in range(MOE_TOP_K):
        v = jnp.max(logits, axis=-1, keepdims=True)
        ix = jnp.min(jnp.where(logits == v, lane, LANES), axis=-1, keepdims=True)
        vals.append(v)
        idxs.append(ix)
        logits = jnp.where(lane == ix, -jnp.inf, logits)
    exps = [jnp.exp(v - vals[0]) for v in vals]
    denom = sum(exps)
    e_out = jnp.zeros(lane.shape, jnp.int32)
    g_out = jnp.zeros(lane.shape, F32)
    for kk in range(MOE_TOP_K):
        e_out = jnp.where(lane == kk, idxs[kk], e_out)
        g_out = jnp.where(lane == kk, exps[kk] / denom, g_out)
    e_ref[...] = e_out
    g_ref[...] = g_out


_HI16 = 0xFFFF0000


def _pack_bf16_halves(y):
    half = y.shape[1] // 2
    bits = lambda v: lax.bitcast_convert_type(v.astype(BF16).astype(F32), jnp.uint32)
    return (bits(y[:, :half]) >> 16) | (bits(y[:, half:]) & jnp.uint32(_HI16))


def _unpack_bf16_halves(pk):
    lo = lax.bitcast_convert_type(pk << 16, F32).astype(BF16)
    hi = lax.bitcast_convert_type(pk & jnp.uint32(_HI16), F32).astype(BF16)
    return lo, hi


def _outproj_ln_kernel(*refs, n_in):
    o_refs, w_refs = refs[:n_in], refs[n_in:2 * n_in]
    x_ref, g_ref, b_ref, rw_ref, rb_ref, y_ref, pk_ref, e_ref, gt_ref = refs[2 * n_in:]
    h = sum(_dot(o[...].astype(BF16), w[...]) for o, w in zip(o_refs, w_refs))
    y = _layer_norm(ALPHA * x_ref[...] + h, g_ref[...], b_ref[...])
    y_ref[...] = y
    pk_ref[...] = _pack_bf16_halves(y)
    _route_topk(y, rw_ref, rb_ref, e_ref, gt_ref)


def _outproj_ln(outs, ws, x, g, b, rw, rb):
    t, d = x.shape
    tm = LN_TILE_M
    n_in = len(outs)
    row = lambda width: pl.BlockSpec((tm, width), lambda i: (i, 0))
    full = lambda a: pl.BlockSpec(a.shape, lambda i: (0,) * a.ndim)
    return pl.pallas_call(
        functools.partial(_outproj_ln_kernel, n_in=n_in),
        out_shape=(jax.ShapeDtypeStruct((t, d), F32),
                   jax.ShapeDtypeStruct((t, d // 2), jnp.uint32),
                   jax.ShapeDtypeStruct((t, LANES), jnp.int32),
                   jax.ShapeDtypeStruct((t, LANES), F32)),
        grid=(t // tm,),
        in_specs=([row(o.shape[1]) for o in outs] + [full(w) for w in ws]
                  + [row(d), full(g), full(b), full(rw), full(rb)]),
        out_specs=(row(d), row(d // 2), row(LANES), row(LANES)),
        compiler_params=_params("parallel"),
        name="outproj_layernorm_router",
    )(*outs, *ws, x, g, b, rw, rb)


def _rms_norm(x, g):
    return x * lax.rsqrt(jnp.mean(x * x, axis=-1, keepdims=True) + RMS_EPS) * g


def _mla_down_kernel(x_ref, w_ref, qn_ref, kvn_ref, cq_ref, ckv_ref, kr_ref):
    a = _dot(x_ref[...].astype(BF16), w_ref[...])
    cq_ref[...] = _rms_norm(a[:, :MLA_Q_RANK], qn_ref[...]).astype(BF16)
    ckv_ref[...] = _rms_norm(a[:, MLA_Q_RANK:MLA_Q_RANK + MLA_KV_RANK],
                             kvn_ref[...]).astype(BF16)
    kr_ref[...] = a[:, MLA_Q_RANK + MLA_KV_RANK:]


def _mla_down(x, w_cat, q_norm, kv_norm):
    t, d = x.shape
    tm = MM_TILE_M
    row = lambda width: pl.BlockSpec((tm, width), lambda i: (i, 0))
    full = lambda a: pl.BlockSpec(a.shape, lambda i: (0,) * a.ndim)
    return pl.pallas_call(
        _mla_down_kernel,
        out_shape=(jax.ShapeDtypeStruct((t, MLA_Q_RANK), BF16),
                   jax.ShapeDtypeStruct((t, MLA_KV_RANK), BF16),
                   jax.ShapeDtypeStruct((t, LANES), F32)),
        grid=(t // tm,),
        in_specs=[row(d), full(w_cat), full(q_norm), full(kv_norm)],
        out_specs=(row(MLA_Q_RANK), row(MLA_KV_RANK), row(LANES)),
        compiler_params=_params("parallel"),
        name="mla_down_rmsnorm",
    )(x, w_cat, q_norm, kv_norm)


def _mla_attn_kernel(cq_ref, ckv_ref, kr_ref, wq_ref, wkv_ref, cos_ref, sin_ref,
                     o_ref, qs, ks, vs, *, seq):
    tile = ATT_TILE_Q
    scale = (MLA_NOPE_DIM + MLA_ROPE_DIM) ** -0.5
    cos, sin = cos_ref[...], sin_ref[...]
    q = _dot(cq_ref[0], wq_ref[...])
    kv = _dot(ckv_ref[0], wkv_ref[...])
    qs[:, :LANES] = (q[:, :LANES] * scale).astype(BF16)
    qs[:, LANES:] = (_rope(q[:, LANES:], cos, sin) * scale).astype(BF16)
    ks[:, :LANES] = kv[:, :MLA_NOPE_DIM].astype(BF16)
    ks[:, LANES:] = _rope(kr_ref[0], cos, sin).astype(BF16)
    vs[...] = kv[:, MLA_NOPE_DIM:].astype(BF16)
    row = lax.broadcasted_iota(jnp.int32, (tile, tile), 0)
    col = lax.broadcasted_iota(jnp.int32, (tile, tile), 1)
    causal = col <= row
    for i in range(seq // tile):
        blocks = [(i * tile, causal)] + [(n * tile, None) for n in range(i)]
        o_ref[0, pl.ds(i * tile, tile), :] = _softmax_blocks(
            qs[pl.ds(i * tile, tile), :], ks, vs, blocks, tile)


def _mla_attn(cq, ckv, kr, wq, wkv, cos, sin):
    b, s, _ = cq.shape
    per_b = lambda width: pl.BlockSpec((1, s, width), lambda i, h: (i, 0, 0))
    head_w = lambda a: pl.BlockSpec((a.shape[0], 2 * LANES), lambda i, h: (0, h))
    tab = pl.BlockSpec((s, LANES), lambda i, h: (0, 0))
    return pl.pallas_call(
        functools.partial(_mla_attn_kernel, seq=s),
        out_shape=jax.ShapeDtypeStruct((b, s, MLA_HEADS * MLA_V_DIM), F32),
        grid=(b, MLA_HEADS),
        in_specs=[per_b(MLA_Q_RANK), per_b(MLA_KV_RANK), per_b(LANES),
                  head_w(wq), head_w(wkv), tab, tab],
        out_specs=pl.BlockSpec((1, s, MLA_V_DIM), lambda i, h: (i, 0, h)),
        scratch_shapes=[pltpu.VMEM((s, 2 * LANES), BF16),
                        pltpu.VMEM((s, 2 * LANES), BF16),
                        pltpu.VMEM((s, MLA_V_DIM), BF16)],
        compiler_params=_params("parallel", "arbitrary"),
        name="mla_attention",
    )(cq, ckv, kr, wq, wkv, cos, sin)


def _start_row_gather(idx_hbm_row, idx_smem, src_hbm, n, sem_idx, sem_rows, dst_of):
    cp = pltpu.make_async_copy(idx_hbm_row, idx_smem, sem_idx)
    cp.start()
    cp.wait()

    def issue(c, carry):
        for u in range(DMA_UNROLL):
            a = c * DMA_UNROLL + u
            pltpu.make_async_copy(src_hbm.at[pl.ds(idx_smem[a], 1)], dst_of(a),
                                  sem_rows).start()
        return carry
    lax.fori_loop(0, n // DMA_UNROLL, issue, 0)


def _for_valid_subs(ns, tm, fn, otherwise=None):
    for s in range(tm // MOE_SUB_M):
        pl.when(s < ns)(functools.partial(fn, s))
        if otherwise is not None:
            pl.when(s >= ns)(functools.partial(otherwise, s))


def _moe_gu_kernel(te_ref, nu_ref, ns_ref, tok_hbm, x_hbm, wgu_ref, bgu_ref, cmp_ref,
                   o_ref, tok_smem, xbuf, xb, wgu_b, sem_idx, sem_rows):
    t, j = pl.program_id(0), pl.program_id(1)
    _, tm, half = xbuf.shape
    sub = MOE_SUB_M
    two_f = wgu_b.shape[1]
    sel = cmp_ref.shape[0]
    used = t < nu_ref[0]
    slot = t % 2

    def start_gather(tile, sl):
        _start_row_gather(tok_hbm.at[tile], tok_smem, x_hbm, ns_ref[tile] * sub,
                          sem_idx, sem_rows.at[sl],
                          lambda a: xbuf.at[sl, pl.ds(a, 1)])

    def blank(s):
        o_ref[pl.ds(s * sub, sub), :] = jnp.zeros((sub, o_ref.shape[1]), o_ref.dtype)

    @pl.when(jnp.logical_and(used, j == 0))
    def _():
        @pl.when(t == 0)
        def _():
            start_gather(0, 0)

        def wait_rows(s):
            pltpu.make_async_copy(x_hbm.at[pl.ds(0, sub)],
                                  xbuf.at[slot, pl.ds(s * sub, sub)],
                                  sem_rows.at[slot]).wait()

        _for_valid_subs(ns_ref[t], tm, wait_rows)

        def land(s):
            rows = pl.ds(s * sub, sub)
            lo, hi = _unpack_bf16_halves(xbuf[slot, rows, :])
            xb[rows, :half] = lo
            xb[rows, half:] = hi

        _for_valid_subs(ns_ref[t], tm, land)

        @pl.when(t + 1 < nu_ref[0])
        def _():
            start_gather(t + 1, 1 - slot)

    @pl.when(jnp.logical_not(used))
    def _():
        o_ref[...] = jnp.zeros(o_ref.shape, o_ref.dtype)

    @pl.when(used)
    def _():
        wgu_b[...] = wgu_ref[0, 0].astype(BF16)
        lane = lax.broadcasted_iota(jnp.int32, (sub, two_f), 1)
        is_glu = (lane & 1) == 0

        def compute(s):
            rows = pl.ds(s * sub, sub)
            h = _dot(xb[rows, :], wgu_b[...]) + bgu_ref[0, 0]
            glu = jnp.minimum(h, SWIGLU_LIMIT)
            lin = jnp.clip(h, -SWIGLU_LIMIT, SWIGLU_LIMIT) + 1.0
            lin = jnp.concatenate(
                [pltpu.roll(lin[:, c:c + LANES], LANES - 1, 1)
                 for c in range(0, two_f, LANES)], axis=1)
            act = jnp.where(is_glu, glu * jax.nn.sigmoid(SWIGLU_ALPHA * glu) * lin, 0.0)
            act = act.astype(BF16)
            o_ref[rows, :] = jnp.concatenate(
                [_dot(act[:, c:c + sel], cmp_ref[...]) for c in range(0, two_f, sel)],
                axis=1).astype(o_ref.dtype)

        _for_valid_subs(ns_ref[t], tm, compute, blank)


def _moe_down_kernel(te_ref, nu_ref, ns_ref, a_ref, wd_ref, bd_ref, o_ref, wd_b):
    t = pl.program_id(0)
    tm = a_ref.shape[0]
    sub = MOE_SUB_M
    used = t < nu_ref[0]

    def blank(s):
        o_ref[pl.ds(s * sub, sub), :] = jnp.zeros((sub, o_ref.shape[1]), F32)

    @pl.when(jnp.logical_not(used))
    def _():
        o_ref[...] = jnp.zeros(o_ref.shape, F32)

    @pl.when(used)
    def _():
        wd_b[...] = wd_ref[0, 0].astype(BF16)

        def compute(s):
            rows = pl.ds(s * sub, sub)
            o_ref[rows, :] = _dot(a_ref[rows, :], wd_b[...]) + bd_ref[0, 0]

        _for_valid_subs(ns_ref[t], tm, compute, blank)


def _moe_experts(x_packed, tok_tiles, tile_expert, n_used, n_sub, w_gu, b_gu, w_down,
                 b_down, layer):
    n_tiles, tm = tok_tiles.shape
    half = x_packed.shape[1]
    d = 2 * half
    ff = w_down.shape[2]
    tf, tn = MOE_TILE_F, MOE_TILE_N
    n_f, n_n = ff // tf, d // tn
    sel = 4 * LANES
    compact = (jnp.arange(sel)[:, None] == 2 * jnp.arange(sel // 2)[None, :]).astype(BF16)

    def tile(i, nu):
        return jnp.minimum(i, nu[0] - 1)

    def chunk(i, j, nu, last):
        return jnp.where(i < nu[0], j, last)

    act = pl.pallas_call(
        _moe_gu_kernel,
        out_shape=jax.ShapeDtypeStruct((n_tiles * tm, ff), BF16),
        grid_spec=pltpu.PrefetchScalarGridSpec(
            num_scalar_prefetch=3,
            grid=(n_tiles, n_f),
            in_specs=[
                pl.BlockSpec(memory_space=pl.ANY),
                pl.BlockSpec(memory_space=pl.ANY),
                pl.BlockSpec((1, 1, d, 2 * tf), lambda i, j, te, nu, ns:
                             (layer, te[tile(i, nu)], 0, chunk(i, j, nu, n_f - 1))),
                pl.BlockSpec((1, 1, 1, 2 * tf), lambda i, j, te, nu, ns:
                             (layer, te[tile(i, nu)], 0, chunk(i, j, nu, n_f - 1))),
                pl.BlockSpec((sel, sel // 2), lambda i, j, te, nu, ns: (0, 0)),
            ],
            out_specs=pl.BlockSpec((tm, tf), lambda i, j, te, nu, ns: (i, j)),
            scratch_shapes=[pltpu.SMEM((tm,), jnp.int32),
                            pltpu.VMEM((2, tm, half), jnp.uint32),
                            pltpu.VMEM((tm, d), BF16),
                            pltpu.VMEM((d, 2 * tf), BF16),
                            pltpu.SemaphoreType.DMA,
                            pltpu.SemaphoreType.DMA((2,))]),
        compiler_params=_params("arbitrary", "arbitrary"),
        name="moe_gate_up",
    )(tile_expert, n_used, n_sub, tok_tiles, x_packed, w_gu, b_gu[:, :, None, :], compact)

    return pl.pallas_call(
        _moe_down_kernel,
        out_shape=jax.ShapeDtypeStruct((n_tiles * tm, d), F32),
        grid_spec=pltpu.PrefetchScalarGridSpec(
            num_scalar_prefetch=3,
            grid=(n_tiles, n_n),
            in_specs=[
                pl.BlockSpec((tm, ff), lambda i, j, te, nu, ns: (tile(i, nu), 0)),
                pl.BlockSpec((1, 1, ff, tn), lambda i, j, te, nu, ns:
                             (layer, te[tile(i, nu)], 0, chunk(i, j, nu, n_n - 1))),
                pl.BlockSpec((1, 1, 1, tn), lambda i, j, te, nu, ns:
                             (layer, te[tile(i, nu)], 0, chunk(i, j, nu, n_n - 1))),
            ],
            out_specs=pl.BlockSpec((tm, tn), lambda i, j, te, nu, ns: (i, j)),
            scratch_shapes=[pltpu.VMEM((ff, tn), BF16)]),
        compiler_params=_params("arbitrary", "arbitrary"),
        name="moe_down",
    )(tile_expert, n_used, n_sub, act, w_down, b_down[:, :, None, :])


def _combine_kernel(pos_hbm, ys_hbm, gates_ref, x_ref, g_ref, b_ref, y_ref,
                    pos_smem, buf, sem_idx, sem_rows):
    i = pl.program_id(0)
    _, k, tc, _ = buf.shape
    slot = i % 2

    def start_gather(tile, sl):
        _start_row_gather(pos_hbm.at[tile], pos_smem, ys_hbm, k * tc, sem_idx,
                          sem_rows.at[sl],
                          lambda a: buf.at[sl, a % k, pl.ds(a // k, 1)])

    @pl.when(i == 0)
    def _():
        start_gather(0, 0)

    for kk in range(k):
        pltpu.make_async_copy(ys_hbm.at[pl.ds(0, tc)], buf.at[slot, kk],
                              sem_rows.at[slot]).wait()

    @pl.when(i + 1 < pl.num_programs(0))
    def _():
        start_gather(i + 1, 1 - slot)

    gates = gates_ref[...]
    h = sum(gates[:, kk:kk + 1] * buf[slot, kk] for kk in range(k))
    y_ref[...] = _layer_norm(ALPHA * x_ref[...] + h, g_ref[...], b_ref[...])


def _combine_ln(pos_tiles, ys, gates, x, g, b):
    t, d = x.shape
    tc = COMBINE_TILE
    row = lambda width: pl.BlockSpec((tc, width), lambda i: (i, 0))
    full = lambda a: pl.BlockSpec(a.shape, lambda i: (0,) * a.ndim)
    anyspec = pl.BlockSpec(memory_space=pl.ANY)
    return pl.pallas_call(
        _combine_kernel,
        out_shape=jax.ShapeDtypeStruct((t, d), F32),
        grid=(t // tc,),
        in_specs=[anyspec, anyspec, row(LANES), row(d), full(g), full(b)],
        out_specs=row(d),
        scratch_shapes=[pltpu.SMEM((MOE_TOP_K * tc,), jnp.int32),
                        pltpu.VMEM((2, MOE_TOP_K, tc, d), F32),
                        pltpu.SemaphoreType.DMA,
                        pltpu.SemaphoreType.DMA((2,))],
        compiler_params=_params("arbitrary"),
        name="moe_combine_layernorm",
    )(pos_tiles, ys, gates, x, g, b)


def _rope_tables(seq, dim, slab):
    half = dim // 2
    pos = jnp.arange(seq, dtype=F32)
    inv = 1.0 / (ROPE_THETA ** (jnp.arange(0, dim, 2, dtype=F32) / dim))
    ang = pos[:, None] * inv[None, :]
    pad = jnp.zeros((seq, slab // 2 - half), F32)
    cos = jnp.concatenate([jnp.cos(ang), pad, jnp.cos(ang), pad], axis=1)
    sin = jnp.concatenate([-jnp.sin(ang), pad, jnp.sin(ang), pad], axis=1)
    return cos, sin


def _pad_rope_cols(w):
    half = MLA_ROPE_DIM // 2
    z = jnp.zeros(w.shape[:-1] + (LANES // 2 - half,), w.dtype)
    return jnp.concatenate([w[..., :half], z, w[..., half:], z], axis=-1)


def _route(top_e, tm, sub):
    t = top_e.shape[0]
    i32 = jnp.int32
    flat_e = top_e.reshape(-1)
    n_assign = t * MOE_TOP_K
    n_tiles = n_assign // tm + N_EXPERTS
    order = jnp.argsort(flat_e).astype(i32)
    sorted_e = flat_e[order]
    counts = jnp.sum(flat_e[:, None] == jnp.arange(N_EXPERTS, dtype=i32)[None, :],
                     axis=0, dtype=i32)
    padded = (counts + tm - 1) // tm * tm
    start = jnp.cumsum(counts) - counts
    pad_end = jnp.cumsum(padded)
    pad_start = pad_end - padded
    tile_start = jnp.arange(n_tiles, dtype=i32) * tm
    tile_expert = jnp.minimum(
        jnp.sum(pad_end[None, :] <= tile_start[:, None], axis=1, dtype=i32), N_EXPERTS - 1)
    n_used = (pad_end[-1:] // tm).astype(i32)
    first = tile_start - pad_start[tile_expert]
    tile_rows = jnp.clip(counts[tile_expert] - first, 0, tm)
    n_sub = (tile_rows + sub - 1) // sub
    k_in_e = first[:, None] + jnp.arange(tm, dtype=i32)[None, :]
    valid = k_in_e < counts[tile_expert][:, None]
    src = jnp.clip(start[tile_expert][:, None] + k_in_e, 0, n_assign - 1)
    row_token = jnp.where(valid, order[src] // MOE_TOP_K, 0).astype(i32)
    dest = (pad_start[sorted_e] + jnp.arange(n_assign, dtype=i32) - start[sorted_e]).astype(i32)
    _, pos = lax.sort_key_val(order, dest)
    return row_token, pos, tile_expert, n_used, n_sub.astype(i32)


def _moe_layer(x_packed, top_e, w_gu, b_gu, w_down, b_down, layer):
    tok_tiles, pos, tile_expert, n_used, n_sub = _route(top_e, MOE_TILE_M, MOE_SUB_M)
    ys = _moe_experts(x_packed, tok_tiles, tile_expert, n_used, n_sub,
                      w_gu, b_gu, w_down, b_down, layer)
    pos_tiles = pos.reshape(-1, COMBINE_TILE * MOE_TOP_K)
    return ys, pos_tiles


def _router_params(router_w, router_b):
    rw = jnp.pad(router_w, ((0, 0), (0, LANES - N_EXPERTS)))
    rw_hi = lax.reduce_precision(rw, exponent_bits=8, mantissa_bits=7)
    rw_lo = rw - rw_hi
    rb = jnp.pad(router_b, (0, LANES - N_EXPERTS), constant_values=MASK_VALUE)
    return jnp.concatenate([rw_hi, rw_lo], axis=1).astype(BF16), rb[None, :]


def kernel(x, ln_mix_g, ln_mix_b, ln_ffn_g, ln_ffn_b, w_in_ab, w_out_ab,
           mla_w_dq, mla_q_norm, mla_w_uq, mla_w_dkv, mla_kv_norm, mla_w_ukv, mla_w_out,
           router_w, router_b, moe_w_gu, moe_b_gu, moe_w_down, moe_b_down):
    b, s, d = x.shape
    t = b * s
    xt = x.reshape(t, d)
    cos_h, sin_h = _rope_tables(s, HEAD_DIM, LANES)
    cos_r, sin_r = _rope_tables(s, MLA_ROPE_DIM, LANES)
    row2 = lambda v: v[None, :]

    for layer in range(DEPTH):
        i = layer // 2
        rw, rb = _router_params(router_w[layer], router_b[layer])
        if layer % 2 == 0:
            proj = _matmul(xt, w_in_ab[i].astype(BF16)).reshape(b, s, -1)
            o_a = _longnet(proj, cos_h, sin_h).reshape(t, -1)
            o_b = _moba(proj, cos_h, sin_h).reshape(t, -1)
            w_out = w_out_ab[i].astype(BF16)
            split = A_HEADS * HEAD_DIM
            outs, ws = [o_a, o_b], [w_out[:split], w_out[split:]]
        else:
            w_cat = jnp.concatenate(
                [mla_w_dq[i], mla_w_dkv[i][:, :MLA_KV_RANK],
                 _pad_rope_cols(mla_w_dkv[i][:, MLA_KV_RANK:])], axis=1).astype(BF16)
            cq, ckv, kr = _mla_down(xt, w_cat, row2(mla_q_norm[i]), row2(mla_kv_norm[i]))
            wq = mla_w_uq[i].reshape(MLA_Q_RANK, MLA_HEADS, MLA_NOPE_DIM + MLA_ROPE_DIM)
            wq = jnp.concatenate([wq[..., :MLA_NOPE_DIM],
                                  _pad_rope_cols(wq[..., MLA_NOPE_DIM:])], axis=-1)
            wq = wq.reshape(MLA_Q_RANK, MLA_HEADS * 2 * LANES).astype(BF16)
            o = _mla_attn(cq.reshape(b, s, -1), ckv.reshape(b, s, -1), kr.reshape(b, s, -1),
                          wq, mla_w_ukv[i].astype(BF16), cos_r, sin_r)
            outs, ws = [o.reshape(t, -1)], [mla_w_out[i].astype(BF16)]
        xt, x_packed, top_e, gates = _outproj_ln(outs, ws, xt, row2(ln_mix_g[layer]),
                                                 row2(ln_mix_b[layer]), rw, rb)
        ys, pos_tiles = _moe_layer(x_packed, top_e[:, :MOE_TOP_K], moe_w_gu, moe_b_gu,
                                   moe_w_down, moe_b_down, layer)
        xt = _combine_ln(pos_tiles, ys, gates, xt,
                         row2(ln_ffn_g[layer]), row2(ln_ffn_b[layer]))
    return xt.reshape(b, s, d)
```

```python
import functools

import jax
import jax.numpy as jnp
from jax import lax
from jax.experimental import pallas as pl
from jax.experimental.pallas import tpu as pltpu

F32 = jnp.float32
BF16 = jnp.bfloat16

DEPTH = 2
HEAD_DIM = 128
A_HEADS = 12
B_HEADS = 4
DILATED_CONFIGS = ((128, 1), (512, 4), (2048, 16))
MOBA_BLOCK = 256
MOBA_TOP_K = 3
ROPE_THETA = 10000.0
MLA_HEADS = 16
MLA_Q_RANK = 512
MLA_KV_RANK = 512
MLA_NOPE_DIM = 128
MLA_ROPE_DIM = 64
MLA_V_DIM = 128
N_EXPERTS = 32
MOE_TOP_K = 4
SWIGLU_LIMIT = 7.0
SWIGLU_ALPHA = 1.702
LN_EPS = 1e-5
RMS_EPS = 1e-6
ALPHA = (2.0 * DEPTH) ** 0.25

LANES = 128
MASK_VALUE = -1e30
VMEM_LIMIT = 56 * 1024 * 1024

MM_TILE_M = 512
MM_TILE_N = 1024
LN_TILE_M = 256
ATT_TILE_Q = 256
MOE_TILE_M = 1024
MOE_SUB_M = 256
MOE_TILE_F = 512
MOE_TILE_N = 1024
COMBINE_TILE = 256
DMA_UNROLL = 8


def _params(*semantics):
    return pltpu.CompilerParams(dimension_semantics=semantics,
                                vmem_limit_bytes=VMEM_LIMIT)


def _dot(a, b):
    return jnp.dot(a, b, preferred_element_type=F32)


def _dot_nt(a, b):
    return lax.dot_general(a, b, (((1,), (1,)), ((), ())),
                           preferred_element_type=F32)


def _dot_nt_f32(a, b):
    return lax.dot_general(a, b, (((1,), (1,)), ((), ())),
                           precision=lax.Precision.HIGHEST,
                           preferred_element_type=F32)


def _rope(x, cos, sin):
    return x * cos + pltpu.roll(x, LANES // 2, 1) * sin


def _mm_kernel(x_ref, w_ref, o_ref, xb_ref):
    @pl.when(pl.program_id(1) == 0)
    def _():
        xb_ref[...] = x_ref[...].astype(BF16)
    o_ref[...] = _dot(xb_ref[...], w_ref[...])


def _matmul(x, w):
    m, k = x.shape
    n = w.shape[1]
    tm, tn = min(MM_TILE_M, m), min(MM_TILE_N, n)
    return pl.pallas_call(
        _mm_kernel,
        out_shape=jax.ShapeDtypeStruct((m, n), F32),
        grid=(m // tm, n // tn),
        in_specs=[pl.BlockSpec((tm, k), lambda i, j: (i, 0)),
                  pl.BlockSpec((k, tn), lambda i, j: (0, j))],
        out_specs=pl.BlockSpec((tm, tn), lambda i, j: (i, j)),
        scratch_shapes=[pltpu.VMEM((tm, k), BF16)],
        compiler_params=_params("parallel", "arbitrary"),
        name="proj_matmul",
    )(x, w)


def _longnet_kernel(q_ref, k_ref, v_ref, cos_ref, sin_ref, o_ref,
                    qs, ks, vs, oc, lc, *, seq):
    band = DILATED_CONFIGS[0][0] // DILATED_CONFIGS[0][1]
    cos, sin = cos_ref[...], sin_ref[...]
    qs[...] = _rope(q_ref[0], cos, sin) * (HEAD_DIM ** -0.5)
    ks[...] = _rope(k_ref[0], cos, sin)
    vs[...] = v_ref[0]
    row = lax.broadcasted_iota(jnp.int32, (band, 2 * band), 0)
    col = lax.broadcasted_iota(jnp.int32, (band, 2 * band), 1)
    band_mask = jnp.logical_and(col >= row, col <= row + band)
    own_mask = (lax.broadcasted_iota(jnp.int32, (band, band), 1)
                <= lax.broadcasted_iota(jnp.int32, (band, band), 0))

    for c, (window, dil) in enumerate(DILATED_CONFIGS):
        assert window // dil == band and seq % (dil * band) == 0
        nb = seq // (dil * band)
        for r in range(dil):
            for n in range(nb):
                def rows(start, size):
                    if dil == 1:
                        return pl.ds(start, size)
                    return pl.ds(start, size, stride=dil)
                start = r + dil * band * n
                cur = rows(start, band)
                keys = cur if n == 0 else rows(start - dil * band, 2 * band)
                mask = own_mask if n == 0 else band_mask
                qb = qs[cur, :].astype(BF16)
                kb = ks[keys, :].astype(BF16)
                vb = vs[keys, :].astype(BF16)
                s = jnp.where(mask, _dot_nt(qb, kb), MASK_VALUE)
                m = jnp.max(s, axis=-1, keepdims=True)
                p = jnp.exp(s - m)
                l = jnp.sum(p, axis=-1, keepdims=True)
                acc = _dot(p.astype(BF16), vb)
                oc[c, cur, :] = acc / l
                lc[c, cur, :] = jnp.broadcast_to(m + jnp.log(l), (band, LANES))

    lses = [lc[c] for c in range(len(DILATED_CONFIGS))]
    top = functools.reduce(jnp.maximum, lses)
    es = [jnp.exp(v - top) for v in lses]
    num = sum(e * oc[c] for c, e in enumerate(es))
    o_ref[0] = num / sum(es)


def _longnet(proj, cos, sin):
    b, s, _ = proj.shape
    n_cfg = len(DILATED_CONFIGS)
    blk = lambda off: pl.BlockSpec((1, s, HEAD_DIM), lambda i, h: (i, 0, off + h))
    tab = pl.BlockSpec((s, HEAD_DIM), lambda i, h: (0, 0))
    return pl.pallas_call(
        functools.partial(_longnet_kernel, seq=s),
        out_shape=jax.ShapeDtypeStruct((b, s, A_HEADS * HEAD_DIM), F32),
        grid=(b, A_HEADS),
        in_specs=[blk(0), blk(A_HEADS), blk(2 * A_HEADS), tab, tab],
        out_specs=pl.BlockSpec((1, s, HEAD_DIM), lambda i, h: (i, 0, h)),
        scratch_shapes=[pltpu.VMEM((s, HEAD_DIM), F32),
                        pltpu.VMEM((s, HEAD_DIM), F32),
                        pltpu.VMEM((s, HEAD_DIM), F32),
                        pltpu.VMEM((n_cfg, s, HEAD_DIM), F32),
                        pltpu.VMEM((n_cfg, s, LANES), F32)],
        compiler_params=_params("parallel", "parallel"),
        name="longnet_attention",
    )(proj, proj, proj, cos, sin)


def _softmax_blocks(qb, k_ref, v_ref, blocks, tile):
    scores = []
    m = None
    for start, mask in blocks:
        s = _dot_nt(qb, k_ref[pl.ds(start, tile), :])
        if mask is not None:
            s = jnp.where(mask, s, MASK_VALUE)
        scores.append(s)
        bm = jnp.max(s, axis=-1, keepdims=True)
        m = bm if m is None else jnp.maximum(m, bm)
    l = None
    acc = None
    for (start, _), s in zip(blocks, scores):
        p = jnp.exp(s - m)
        pl_sum = jnp.sum(p, axis=-1, keepdims=True)
        pv = _dot(p.astype(BF16), v_ref[pl.ds(start, tile), :])
        l = pl_sum if l is None else l + pl_sum
        acc = pv if acc is None else acc + pv
    return acc / l


def _moba_kernel(q_ref, k_ref, v_ref, cos_ref, sin_ref, o_ref, qs, ks, vs, *, seq):
    blk = MOBA_BLOCK
    nblk = seq // blk
    cos, sin = cos_ref[...], sin_ref[...]
    q = _rope(q_ref[0], cos, sin)
    k = _rope(k_ref[0], cos, sin)
    qs[...] = (q * (HEAD_DIM ** -0.5)).astype(BF16)
    ks[...] = k.astype(BF16)
    vs[...] = v_ref[0].astype(BF16)
    k_mean = jnp.concatenate(
        [jnp.mean(k[n * blk:(n + 1) * blk], axis=0, keepdims=True) for n in range(nblk)],
        axis=0)
    gate = _dot_nt_f32(q, k_mean)
    row = lax.broadcasted_iota(jnp.int32, (blk, blk), 0)
    col = lax.broadcasted_iota(jnp.int32, (blk, blk), 1)
    causal = col <= row
    n_top = min(MOBA_TOP_K, nblk)

    for i in range(nblk):
        g = gate[i * blk:(i + 1) * blk, :]
        blocks = [(i * blk, causal)]
        for n in range(i):
            if i <= n_top:
                blocks.append((n * blk, None))
                continue
            gn = g[:, n:n + 1]
            rank = jnp.zeros((blk, 1), jnp.int32)
            for o in range(i):
                if o == n:
                    continue
                go = g[:, o:o + 1]
                ahead = (go > gn) | (go == gn) if o < n else (go > gn)
                rank = rank + ahead.astype(jnp.int32)
            blocks.append((n * blk, rank < n_top))
        o_ref[0, pl.ds(i * blk, blk), :] = _softmax_blocks(
            qs[pl.ds(i * blk, blk), :], ks, vs, blocks, blk)


def _moba(proj, cos, sin):
    b, s, _ = proj.shape
    base = 3 * A_HEADS
    blk = lambda off: pl.BlockSpec((1, s, HEAD_DIM), lambda i, h: (i, 0, base + off + h))
    tab = pl.BlockSpec((s, HEAD_DIM), lambda i, h: (0, 0))
    return pl.pallas_call(
        functools.partial(_moba_kernel, seq=s),
        out_shape=jax.ShapeDtypeStruct((b, s, B_HEADS * HEAD_DIM), F32),
        grid=(b, B_HEADS),
        in_specs=[blk(0), blk(B_HEADS), blk(2 * B_HEADS), tab, tab],
        out_specs=pl.BlockSpec((1, s, HEAD_DIM), lambda i, h: (i, 0, h)),
        scratch_shapes=[pltpu.VMEM((s, HEAD_DIM), BF16)] * 3,
        compiler_params=_params("parallel", "parallel"),
        name="moba_attention",
    )(proj, proj, proj, cos, sin)


def _layer_norm(z, g, b):
    mu = jnp.mean(z, axis=-1, keepdims=True)
    zc = z - mu
    var = jnp.mean(zc * zc, axis=-1, keepdims=True)
    return zc * lax.rsqrt(var + LN_EPS) * g + b


def _route_topk(y, rw_ref, rb_ref, e_ref, g_ref):
    y_hi = y.astype(BF16)
    y_lo = (y - y_hi.astype(F32)).astype(BF16)
    hi_both = _dot(y_hi, rw_ref[...])
    lo_hi = _dot(y_lo, rw_ref[:, :LANES])
    logits = hi_both[:, :LANES] + (hi_both[:, LANES:] + lo_hi) + rb_ref[...]
    lane = lax.broadcasted_iota(jnp.int32, logits.shape, 1)
    vals, idxs = [], []
    for _ in range(MOE_TOP_K):
        v = jnp.max(logits, axis=-1, keepdims=True)
        ix = jnp.min(jnp.where(logits == v, lane, LANES), axis=-1, keepdims=True)
        vals.append(v)
        idxs.append(ix)
        logits = jnp.where(lane == ix, -jnp.inf, logits)
    exps = [jnp.exp(v - vals[0]) for v in vals]
    denom = sum(exps)
    e_out = jnp.zeros(lane.shape, jnp.int32)
    g_out = jnp.zeros(lane.shape, F32)
    for kk in range(MOE_TOP_K):
        e_out = jnp.where(lane == kk, idxs[kk], e_out)
        g_out = jnp.where(lane == kk, exps[kk] / denom, g_out)
    e_ref[...] = e_out
    g_ref[...] = g_out


_HI16 = 0xFFFF0000


def _pack_bf16_halves(y):
    half = y.shape[1] // 2
    bits = lambda v: lax.bitcast_convert_type(v.astype(BF16).astype(F32), jnp.uint32)
    return (bits(y[:, :half]) >> 16) | (bits(y[:, half:]) & jnp.uint32(_HI16))


def _unpack_bf16_halves(pk):
    lo = lax.bitcast_convert_type(pk << 16, F32).astype(BF16)
    hi = lax.bitcast_convert_type(pk & jnp.uint32(_HI16), F32).astype(BF16)
    return lo, hi


def _outproj_ln_kernel(*refs, n_in):
    o_refs, w_refs = refs[:n_in], refs[n_in:2 * n_in]
    x_ref, g_ref, b_ref, rw_ref, rb_ref, y_ref, pk_ref, e_ref, gt_ref = refs[2 * n_in:]
    h = sum(_dot(o[...].astype(BF16), w[...]) for o, w in zip(o_refs, w_refs))
    y = _layer_norm(ALPHA * x_ref[...] + h, g_ref[...], b_ref[...])
    y_ref[...] = y
    pk_ref[...] = _pack_bf16_halves(y)
    _route_topk(y, rw_ref, rb_ref, e_ref, gt_ref)


def _outproj_ln(outs, ws, x, g, b, rw, rb):
    t, d = x.shape
    tm = LN_TILE_M
    n_in = len(outs)
    row = lambda width: pl.BlockSpec((tm, width), lambda i: (i, 0))
    full = lambda a: pl.BlockSpec(a.shape, lambda i: (0,) * a.ndim)
    return pl.pallas_call(
        functools.partial(_outproj_ln_kernel, n_in=n_in),
        out_shape=(jax.ShapeDtypeStruct((t, d), F32),
                   jax.ShapeDtypeStruct((t, d // 2), jnp.uint32),
                   jax.ShapeDtypeStruct((t, LANES), jnp.int32),
                   jax.ShapeDtypeStruct((t, LANES), F32)),
        grid=(t // tm,),
        in_specs=([row(o.shape[1]) for o in outs] + [full(w) for w in ws]
                  + [row(d), full(g), full(b), full(rw), full(rb)]),
        out_specs=(row(d), row(d // 2), row(LANES), row(LANES)),
        compiler_params=_params("parallel"),
        name="outproj_layernorm_router",
    )(*outs, *ws, x, g, b, rw, rb)


def _rms_norm(x, g):
    return x * lax.rsqrt(jnp.mean(x * x, axis=-1, keepdims=True) + RMS_EPS) * g


def _mla_down_kernel(x_ref, w_ref, qn_ref, kvn_ref, cq_ref, ckv_ref, kr_ref):
    a = _dot(x_ref[...].astype(BF16), w_ref[...])
    cq_ref[...] = _rms_norm(a[:, :MLA_Q_RANK], qn_ref[...]).astype(BF16)
    ckv_ref[...] = _rms_norm(a[:, MLA_Q_RANK:MLA_Q_RANK + MLA_KV_RANK],
                             kvn_ref[...]).astype(BF16)
    kr_ref[...] = a[:, MLA_Q_RANK + MLA_KV_RANK:]


def _mla_down(x, w_cat, q_norm, kv_norm):
    t, d = x.shape
    tm = MM_TILE_M
    row = lambda width: pl.BlockSpec((tm, width), lambda i: (i, 0))
    full = lambda a: pl.BlockSpec(a.shape, lambda i: (0,) * a.ndim)
    return pl.pallas_call(
        _mla_down_kernel,
        out_shape=(jax.ShapeDtypeStruct((t, MLA_Q_RANK), BF16),
                   jax.ShapeDtypeStruct((t, MLA_KV_RANK), BF16),
                   jax.ShapeDtypeStruct((t, LANES), F32)),
        grid=(t // tm,),
        in_specs=[row(d), full(w_cat), full(q_norm), full(kv_norm)],
        out_specs=(row(MLA_Q_RANK), row(MLA_KV_RANK), row(LANES)),
        compiler_params=_params("parallel"),
        name="mla_down_rmsnorm",
    )(x, w_cat, q_norm, kv_norm)


def _mla_attn_kernel(cq_ref, ckv_ref, kr_ref, wq_ref, wkv_ref, cos_ref, sin_ref,
                     o_ref, qs, ks, vs, *, seq):
    tile = ATT_TILE_Q
    scale = (MLA_NOPE_DIM + MLA_ROPE_DIM) ** -0.5
    cos, sin = cos_ref[...], sin_ref[...]
    q = _dot(cq_ref[0], wq_ref[...])
    kv = _dot(ckv_ref[0], wkv_ref[...])
    qs[:, :LANES] = (q[:, :LANES] * scale).astype(BF16)
    qs[:, LANES:] = (_rope(q[:, LANES:], cos, sin) * scale).astype(BF16)
    ks[:, :LANES] = kv[:, :MLA_NOPE_DIM].astype(BF16)
    ks[:, LANES:] = _rope(kr_ref[0], cos, sin).astype(BF16)
    vs[...] = kv[:, MLA_NOPE_DIM:].astype(BF16)
    row = lax.broadcasted_iota(jnp.int32, (tile, tile), 0)
    col = lax.broadcasted_iota(jnp.int32, (tile, tile), 1)
    causal = col <= row
    for i in range(seq // tile):
        blocks = [(i * tile, causal)] + [(n * tile, None) for n in range(i)]
        o_ref[0, pl.ds(i * tile, tile), :] = _softmax_blocks(
            qs[pl.ds(i * tile, tile), :], ks, vs, blocks, tile)


def _mla_attn(cq, ckv, kr, wq, wkv, cos, sin):
    b, s, _ = cq.shape
    per_b = lambda width: pl.BlockSpec((1, s, width), lambda i, h: (i, 0, 0))
    head_w = lambda a: pl.BlockSpec((a.shape[0], 2 * LANES), lambda i, h: (0, h))
    tab = pl.BlockSpec((s, LANES), lambda i, h: (0, 0))
    return pl.pallas_call(
        functools.partial(_mla_attn_kernel, seq=s),
        out_shape=jax.ShapeDtypeStruct((b, s, MLA_HEADS * MLA_V_DIM), F32),
        grid=(b, MLA_HEADS),
        in_specs=[per_b(MLA_Q_RANK), per_b(MLA_KV_RANK), per_b(LANES),
                  head_w(wq), head_w(wkv), tab, tab],
        out_specs=pl.BlockSpec((1, s, MLA_V_DIM), lambda i, h: (i, 0, h)),
        scratch_shapes=[pltpu.VMEM((s, 2 * LANES), BF16),
                        pltpu.VMEM((s, 2 * LANES), BF16),
                        pltpu.VMEM((s, MLA_V_DIM), BF16)],
        compiler_params=_params("parallel", "arbitrary"),
        name="mla_attention",
    )(cq, ckv, kr, wq, wkv, cos, sin)


def _issue_row_gather(idx_of, src_hbm, n, sem_rows, dst_of, alternate=False):
    def issue(c, carry):
        for u in range(DMA_UNROLL):
            pltpu.make_async_copy(src_hbm.at[pl.ds(idx_of(c * DMA_UNROLL + u), 1)],
                                  dst_of(c, u), sem_rows).start(
                                      priority=u % 2 if alternate else 0)
        return carry
    lax.fori_loop(0, n // DMA_UNROLL, issue, 0)


def _for_row_counts(ns, tm, fn):
    for n in range(1, tm // MOE_SUB_M + 1):
        pl.when(ns == n)(functools.partial(fn, n * MOE_SUB_M))


def _moe_gu_kernel(te_ref, nu_ref, ns_ref, tok_hbm, x_hbm, wgu_ref, bgu_ref, cmp_ref,
                   o_ref, tok_smem, xbuf, xb, wgu_b, sem_idx, sem_rows):
    t, j = pl.program_id(0), pl.program_id(1)
    _, tm, half = xbuf.shape
    two_f = wgu_b.shape[1]
    sel = cmp_ref.shape[0]
    used = t < nu_ref[0]
    slot = t % 2

    def start_gather(tile, sl):
        cp = pltpu.make_async_copy(tok_hbm.at[tile], tok_smem, sem_idx)
        cp.start()
        cp.wait()
        _issue_row_gather(lambda a: tok_smem[a], x_hbm, ns_ref[tile] * MOE_SUB_M,
                          sem_rows.at[sl],
                          lambda c, u: xbuf.at[sl, pl.ds(c * DMA_UNROLL + u, 1)])

    @pl.when(jnp.logical_and(used, j == 0))
    def _():
        @pl.when(t == 0)
        def _():
            start_gather(0, 0)

        def land(rows):
            r = pl.ds(0, rows)
            pltpu.make_async_copy(x_hbm.at[r], xbuf.at[slot, r], sem_rows.at[slot]).wait()
            lo, hi = _unpack_bf16_halves(xbuf[slot, r, :])
            xb[r, :half] = lo
            xb[r, half:] = hi

        _for_row_counts(ns_ref[t], tm, land)

        @pl.when(t + 1 < nu_ref[0])
        def _():
            start_gather(t + 1, 1 - slot)

    @pl.when(jnp.logical_not(used))
    def _():
        o_ref[...] = jnp.zeros(o_ref.shape, o_ref.dtype)

    @pl.when(used)
    def _():
        wgu_b[...] = wgu_ref[0, 0].astype(BF16)

        def compute(rows):
            r = pl.ds(0, rows)
            x = xb[r, :]
            is_glu = (lax.broadcasted_iota(jnp.int32, (rows, sel), 1) & 1) == 0
            for c in range(0, two_f, sel):
                h = _dot(x, wgu_b[:, c:c + sel]) + bgu_ref[0, 0, :, c:c + sel]
                glu = jnp.minimum(h, SWIGLU_LIMIT)
                lin = jnp.clip(h, -SWIGLU_LIMIT, SWIGLU_LIMIT) + 1.0
                lin = jnp.concatenate(
                    [pltpu.roll(lin[:, v:v + LANES], LANES - 1, 1)
                     for v in range(0, sel, LANES)], axis=1)
                act = jnp.where(is_glu, glu * jax.nn.sigmoid(SWIGLU_ALPHA * glu) * lin, 0.0)
                o_ref[r, c // 2:(c + sel) // 2] = _dot(
                    act.astype(BF16), cmp_ref[...]).astype(o_ref.dtype)
            if rows < tm:
                o_ref[pl.ds(rows, tm - rows), :] = jnp.zeros(
                    (tm - rows, o_ref.shape[1]), o_ref.dtype)

        _for_row_counts(ns_ref[t], tm, compute)


def _moe_down_kernel(te_ref, nu_ref, ns_ref, a_ref, wd_ref, bd_ref, o_ref, wd_b):
    t = pl.program_id(0)
    tm = a_ref.shape[0]

    @pl.when(t >= nu_ref[0])
    def _():
        o_ref[...] = jnp.zeros(o_ref.shape, F32)

    @pl.when(t < nu_ref[0])
    def _():
        wd_b[...] = wd_ref[0, 0].astype(BF16)

        def compute(rows):
            r = pl.ds(0, rows)
            o_ref[r, :] = _dot(a_ref[r, :], wd_b[...]) + bd_ref[0, 0]
            if rows < tm:
                o_ref[pl.ds(rows, tm - rows), :] = jnp.zeros(
                    (tm - rows, o_ref.shape[1]), F32)

        _for_row_counts(ns_ref[t], tm, compute)


def _moe_experts(x_packed, tok_tiles, tile_expert, n_used, n_sub, w_gu, b_gu, w_down,
                 b_down, layer):
    n_tiles, tm = tok_tiles.shape
    half = x_packed.shape[1]
    d = 2 * half
    ff = w_down.shape[2]
    tf, tn = MOE_TILE_F, MOE_TILE_N
    n_f, n_n = ff // tf, d // tn
    sel = 4 * LANES
    compact = (jnp.arange(sel)[:, None] == 2 * jnp.arange(sel // 2)[None, :]).astype(BF16)

    def tile(i, nu):
        return jnp.minimum(i, nu[0] - 1)

    def chunk(i, j, nu, last):
        return jnp.where(i < nu[0], j, last)

    act = pl.pallas_call(
        _moe_gu_kernel,
        out_shape=jax.ShapeDtypeStruct((n_tiles * tm, ff), BF16),
        grid_spec=pltpu.PrefetchScalarGridSpec(
            num_scalar_prefetch=3,
            grid=(n_tiles, n_f),
            in_specs=[
                pl.BlockSpec(memory_space=pl.ANY),
                pl.BlockSpec(memory_space=pl.ANY),
                pl.BlockSpec((1, 1, d, 2 * tf), lambda i, j, te, nu, ns:
                             (layer, te[tile(i, nu)], 0, chunk(i, j, nu, n_f - 1))),
                pl.BlockSpec((1, 1, 1, 2 * tf), lambda i, j, te, nu, ns:
                             (layer, te[tile(i, nu)], 0, chunk(i, j, nu, n_f - 1))),
                pl.BlockSpec((sel, sel // 2), lambda i, j, te, nu, ns: (0, 0)),
            ],
            out_specs=pl.BlockSpec((tm, tf), lambda i, j, te, nu, ns: (i, j)),
            scratch_shapes=[pltpu.SMEM((tm,), jnp.int32),
                            pltpu.VMEM((2, tm, half), jnp.uint32),
                            pltpu.VMEM((tm, d), BF16),
                            pltpu.VMEM((d, 2 * tf), BF16),
                            pltpu.SemaphoreType.DMA,
                            pltpu.SemaphoreType.DMA((2,))]),
        compiler_params=_params("arbitrary", "arbitrary"),
        name="moe_gate_up",
    )(tile_expert, n_used, n_sub, tok_tiles, x_packed, w_gu, b_gu[:, :, None, :], compact)

    return pl.pallas_call(
        _moe_down_kernel,
        out_shape=jax.ShapeDtypeStruct((n_tiles * tm, d), F32),
        grid_spec=pltpu.PrefetchScalarGridSpec(
            num_scalar_prefetch=3,
            grid=(n_tiles, n_n),
            in_specs=[
                pl.BlockSpec((tm, ff), lambda i, j, te, nu, ns: (tile(i, nu), 0)),
                pl.BlockSpec((1, 1, ff, tn), lambda i, j, te, nu, ns:
                             (layer, te[tile(i, nu)], 0, chunk(i, j, nu, n_n - 1))),
                pl.BlockSpec((1, 1, 1, tn), lambda i, j, te, nu, ns:
                             (layer, te[tile(i, nu)], 0, chunk(i, j, nu, n_n - 1))),
            ],
            out_specs=pl.BlockSpec((tm, tn), lambda i, j, te, nu, ns: (i, j)),
            scratch_shapes=[pltpu.VMEM((ff, tn), BF16)]),
        compiler_params=_params("arbitrary", "arbitrary"),
        name="moe_down",
    )(tile_expert, n_used, n_sub, act, w_down, b_down[:, :, None, :])


def _combine_kernel(pos_hbm, ys_hbm, gates_ref, x_ref, g_ref, b_ref, y_ref,
                    pos_smem, buf, sem_idx, sem_rows):
    i = pl.program_id(0)
    n = pl.num_programs(0)
    _, k, tc, _ = buf.shape
    per_slot = tc // DMA_UNROLL
    slot = i % 2

    def idx_copy(tile, sl):
        return pltpu.make_async_copy(pos_hbm.at[tile], pos_smem.at[sl], sem_idx.at[sl])

    def start_gather(sl):
        _issue_row_gather(
            lambda a: pos_smem[sl, a], ys_hbm, k * tc, sem_rows.at[sl],
            lambda c, u: buf.at[sl, c // per_slot,
                                pl.ds((c % per_slot) * DMA_UNROLL + u, 1)],
            alternate=True)

    @pl.when(i == 0)
    def _():
        idx_copy(0, 0).start()
        idx_copy(0, 0).wait()
        start_gather(0)

        @pl.when(n > 1)
        def _():
            idx_copy(1, 1).start()

    for kk in range(k):
        pltpu.make_async_copy(ys_hbm.at[pl.ds(0, tc)], buf.at[slot, kk],
                              sem_rows.at[slot]).wait()

    @pl.when(i + 1 < n)
    def _():
        idx_copy(i + 1, 1 - slot).wait()
        start_gather(1 - slot)

        @pl.when(i + 2 < n)
        def _():
            idx_copy(i + 2, slot).start()

    gates = gates_ref[...]
    h = sum(gates[:, kk:kk + 1] * buf[slot, kk] for kk in range(k))
    y_ref[...] = _layer_norm(ALPHA * x_ref[...] + h, g_ref[...], b_ref[...])


def _combine_ln(pos_tiles, ys, gates, x, g, b):
    t, d = x.shape
    tc = COMBINE_TILE
    row = lambda width: pl.BlockSpec((tc, width), lambda i: (i, 0))
    full = lambda a: pl.BlockSpec(a.shape, lambda i: (0,) * a.ndim)
    anyspec = pl.BlockSpec(memory_space=pl.ANY)
    return pl.pallas_call(
        _combine_kernel,
        out_shape=jax.ShapeDtypeStruct((t, d), F32),
        grid=(t // tc,),
        in_specs=[anyspec, anyspec, row(LANES), row(d), full(g), full(b)],
        out_specs=row(d),
        scratch_shapes=[pltpu.SMEM((2, MOE_TOP_K * tc), jnp.int32),
                        pltpu.VMEM((2, MOE_TOP_K, tc, d), F32),
                        pltpu.SemaphoreType.DMA((2,)),
                        pltpu.SemaphoreType.DMA((2,))],
        compiler_params=_params("arbitrary"),
        name="moe_combine_layernorm",
    )(pos_tiles, ys, gates, x, g, b)


def _rope_tables(seq, dim, slab):
    half = dim // 2
    pos = jnp.arange(seq, dtype=F32)
    inv = 1.0 / (ROPE_THETA ** (jnp.arange(0, dim, 2, dtype=F32) / dim))
    ang = pos[:, None] * inv[None, :]
    pad = jnp.zeros((seq, slab // 2 - half), F32)
    cos = jnp.concatenate([jnp.cos(ang), pad, jnp.cos(ang), pad], axis=1)
    sin = jnp.concatenate([-jnp.sin(ang), pad, jnp.sin(ang), pad], axis=1)
    return cos, sin


def _pad_rope_cols(w):
    half = MLA_ROPE_DIM // 2
    z = jnp.zeros(w.shape[:-1] + (LANES // 2 - half,), w.dtype)
    return jnp.concatenate([w[..., :half], z, w[..., half:], z], axis=-1)


def _route(top_e, tm, sub):
    t = top_e.shape[0]
    i32 = jnp.int32
    flat_e = top_e.reshape(-1)
    n_assign = t * MOE_TOP_K
    n_tiles = n_assign // tm + N_EXPERTS
    order = jnp.argsort(flat_e).astype(i32)
    sorted_e = flat_e[order]
    counts = jnp.sum(flat_e[:, None] == jnp.arange(N_EXPERTS, dtype=i32)[None, :],
                     axis=0, dtype=i32)
    padded = (counts + tm - 1) // tm * tm
    start = jnp.cumsum(counts) - counts
    pad_end = jnp.cumsum(padded)
    pad_start = pad_end - padded
    tile_start = jnp.arange(n_tiles, dtype=i32) * tm
    tile_expert = jnp.minimum(
        jnp.sum(pad_end[None, :] <= tile_start[:, None], axis=1, dtype=i32), N_EXPERTS - 1)
    n_used = (pad_end[-1:] // tm).astype(i32)
    first = tile_start - pad_start[tile_expert]
    tile_rows = jnp.clip(counts[tile_expert] - first, 0, tm)
    n_sub = (tile_rows + sub - 1) // sub
    k_in_e = first[:, None] + jnp.arange(tm, dtype=i32)[None, :]
    valid = k_in_e < counts[tile_expert][:, None]
    src = jnp.clip(start[tile_expert][:, None] + k_in_e, 0, n_assign - 1)
    row_token = jnp.where(valid, order[src] // MOE_TOP_K, 0).astype(i32)
    dest = (pad_start[sorted_e] + jnp.arange(n_assign, dtype=i32) - start[sorted_e]).astype(i32)
    _, pos = lax.sort_key_val(order, dest)
    return row_token, pos, tile_expert, n_used, n_sub.astype(i32)


def _moe_layer(x_packed, top_e, w_gu, b_gu, w_down, b_down, layer):
    tok_tiles, pos, tile_expert, n_used, n_sub = _route(top_e, MOE_TILE_M, MOE_SUB_M)
    ys = _moe_experts(x_packed, tok_tiles, tile_expert, n_used, n_sub,
                      w_gu, b_gu, w_down, b_down, layer)
    pos_tiles = pos.reshape(-1, COMBINE_TILE, MOE_TOP_K).transpose(0, 2, 1)
    return ys, pos_tiles.reshape(-1, MOE_TOP_K * COMBINE_TILE)


def _router_params(router_w, router_b):
    rw = jnp.pad(router_w, ((0, 0), (0, LANES - N_EXPERTS)))
    rw_hi = lax.reduce_precision(rw, exponent_bits=8, mantissa_bits=7)
    rw_lo = rw - rw_hi
    rb = jnp.pad(router_b, (0, LANES - N_EXPERTS), constant_values=MASK_VALUE)
    return jnp.concatenate([rw_hi, rw_lo], axis=1).astype(BF16), rb[None, :]


def kernel(x, ln_mix_g, ln_mix_b, ln_ffn_g, ln_ffn_b, w_in_ab, w_out_ab,
           mla_w_dq, mla_q_norm, mla_w_uq, mla_w_dkv, mla_kv_norm, mla_w_ukv, mla_w_out,
           router_w, router_b, moe_w_gu, moe_b_gu, moe_w_down, moe_b_down):
    b, s, d = x.shape
    t = b * s
    xt = x.reshape(t, d)
    cos_h, sin_h = _rope_tables(s, HEAD_DIM, LANES)
    cos_r, sin_r = _rope_tables(s, MLA_ROPE_DIM, LANES)
    row2 = lambda v: v[None, :]

    for layer in range(DEPTH):
        i = layer // 2
        rw, rb = _router_params(router_w[layer], router_b[layer])
        if layer % 2 == 0:
            proj = _matmul(xt, w_in_ab[i].astype(BF16)).reshape(b, s, -1)
            o_a = _longnet(proj, cos_h, sin_h).reshape(t, -1)
            o_b = _moba(proj, cos_h, sin_h).reshape(t, -1)
            w_out = w_out_ab[i].astype(BF16)
            split = A_HEADS * HEAD_DIM
            outs, ws = [o_a, o_b], [w_out[:split], w_out[split:]]
        else:
            w_cat = jnp.concatenate(
                [mla_w_dq[i], mla_w_dkv[i][:, :MLA_KV_RANK],
                 _pad_rope_cols(mla_w_dkv[i][:, MLA_KV_RANK:])], axis=1).astype(BF16)
            cq, ckv, kr = _mla_down(xt, w_cat, row2(mla_q_norm[i]), row2(mla_kv_norm[i]))
            wq = mla_w_uq[i].reshape(MLA_Q_RANK, MLA_HEADS, MLA_NOPE_DIM + MLA_ROPE_DIM)
            wq = jnp.concatenate([wq[..., :MLA_NOPE_DIM],
                                  _pad_rope_cols(wq[..., MLA_NOPE_DIM:])], axis=-1)
            wq = wq.reshape(MLA_Q_RANK, MLA_HEADS * 2 * LANES).astype(BF16)
            o = _mla_attn(cq.reshape(b, s, -1), ckv.reshape(b, s, -1), kr.reshape(b, s, -1),
                          wq, mla_w_ukv[i].astype(BF16), cos_r, sin_r)
            outs, ws = [o.reshape(t, -1)], [mla_w_out[i].astype(BF16)]
        xt, x_packed, top_e, gates = _outproj_ln(outs, ws, xt, row2(ln_mix_g[layer]),
                                                 row2(ln_mix_b[layer]), rw, rb)
        ys, pos_tiles = _moe_layer(x_packed, top_e[:, :MOE_TOP_K], moe_w_gu, moe_b_gu,
                                   moe_w_down, moe_b_down, layer)
        xt = _combine_ln(pos_tiles, ys, gates, xt,
                         row2(ln_ffn_g[layer]), row2(ln_ffn_b[layer]))
    return xt.reshape(b, s, d)
```

```python
import functools

import jax
import jax.numpy as jnp
from jax import lax
from jax.experimental import pallas as pl
from jax.experimental.pallas import tpu as pltpu

F32 = jnp.float32
BF16 = jnp.bfloat16

DEPTH = 2
HEAD_DIM = 128
A_HEADS = 12
B_HEADS = 4
DILATED_CONFIGS = ((128, 1), (512, 4), (2048, 16))
MOBA_BLOCK = 256
MOBA_TOP_K = 3
ROPE_THETA = 10000.0
MLA_HEADS = 16
MLA_Q_RANK = 512
MLA_KV_RANK = 512
MLA_NOPE_DIM = 128
MLA_ROPE_DIM = 64
MLA_V_DIM = 128
N_EXPERTS = 32
MOE_TOP_K = 4
SWIGLU_LIMIT = 7.0
SWIGLU_ALPHA = 1.702
LN_EPS = 1e-5
RMS_EPS = 1e-6
ALPHA = (2.0 * DEPTH) ** 0.25

LANES = 128
MASK_VALUE = -1e30
VMEM_LIMIT = 56 * 1024 * 1024

MM_TILE_M = 512
MM_TILE_N = 1024
LN_TILE_M = 256
ATT_TILE_Q = 256
LONGNET_GROUP = 4
MOE_TILE_M = 1024
MOE_SUB_M = 256
MOE_TILE_F = 512
MOE_TILE_N = 1024
COMBINE_TILE = 256
DMA_UNROLL = 8


def _params(*semantics):
    return pltpu.CompilerParams(dimension_semantics=semantics,
                                vmem_limit_bytes=VMEM_LIMIT)


def _dot(a, b):
    return jnp.dot(a, b, preferred_element_type=F32)


def _dot_nt(a, b):
    return lax.dot_general(a, b, (((1,), (1,)), ((), ())),
                           preferred_element_type=F32)


def _dot_nt_f32(a, b):
    return lax.dot_general(a, b, (((1,), (1,)), ((), ())),
                           precision=lax.Precision.HIGHEST,
                           preferred_element_type=F32)


def _rope(x, cos, sin):
    return x * cos + pltpu.roll(x, LANES // 2, 1) * sin


def _mm_kernel(x_ref, w_ref, o_ref, xb_ref):
    @pl.when(pl.program_id(1) == 0)
    def _():
        xb_ref[...] = x_ref[...].astype(BF16)
    o_ref[...] = _dot(xb_ref[...], w_ref[...])


def _matmul(x, w):
    m, k = x.shape
    n = w.shape[1]
    tm, tn = min(MM_TILE_M, m), min(MM_TILE_N, n)
    return pl.pallas_call(
        _mm_kernel,
        out_shape=jax.ShapeDtypeStruct((m, n), F32),
        grid=(m // tm, n // tn),
        in_specs=[pl.BlockSpec((tm, k), lambda i, j: (i, 0)),
                  pl.BlockSpec((k, tn), lambda i, j: (0, j))],
        out_specs=pl.BlockSpec((tm, tn), lambda i, j: (i, j)),
        scratch_shapes=[pltpu.VMEM((tm, k), BF16)],
        compiler_params=_params("parallel", "arbitrary"),
        name="proj_matmul",
    )(x, w)


def _longnet_kernel(q_ref, k_ref, v_ref, cos_ref, sin_ref, o_ref,
                    qs, ks, vs, oc, lc, *, seq):
    band = DILATED_CONFIGS[0][0] // DILATED_CONFIGS[0][1]
    cos, sin = cos_ref[...], sin_ref[...]
    qs[...] = _rope(q_ref[0], cos, sin) * (HEAD_DIM ** -0.5)
    ks[...] = _rope(k_ref[0], cos, sin)
    vs[...] = v_ref[0]

    def band_mask(g, with_prev):
        width = 2 * band if with_prev else band
        qi = lax.broadcasted_iota(jnp.int32, (g, band, width), 1)
        kj = lax.broadcasted_iota(jnp.int32, (g, band, width), 2)
        if with_prev:
            return jnp.logical_and(kj >= qi, kj <= qi + band)
        return kj <= qi

    def rows(dil, start, size):
        return pl.ds(start, size) if dil == 1 else pl.ds(start, size, stride=dil)

    first_blocks, later_blocks = [], []
    for c, (window, dil) in enumerate(DILATED_CONFIGS):
        assert window // dil == band and seq % (dil * band) == 0
        for r in range(dil):
            for n in range(seq // (dil * band)):
                (later_blocks if n else first_blocks).append((c, dil, r + dil * band * n))

    def attend(group, with_prev):
        def stacked(ref, back, size):
            return jnp.stack([ref[rows(d, st - back * d * band, size), :]
                              for _, d, st in group]).astype(BF16)
        qb = stacked(qs, 0, band)
        kb = stacked(ks, 1, 2 * band) if with_prev else stacked(ks, 0, band)
        vb = stacked(vs, 1, 2 * band) if with_prev else stacked(vs, 0, band)
        s = lax.dot_general(qb, kb, (((2,), (2,)), ((0,), (0,))),
                            preferred_element_type=F32)
        s = jnp.where(band_mask(len(group), with_prev), s, MASK_VALUE)
        m = jnp.max(s, axis=-1, keepdims=True)
        p = jnp.exp(s - m)
        l = jnp.sum(p, axis=-1, keepdims=True)
        acc = lax.dot_general(p.astype(BF16), vb, (((2,), (1,)), ((0,), (0,))),
                              preferred_element_type=F32)
        o = acc / l
        lse = m + jnp.log(l)
        for g, (c, d, st) in enumerate(group):
            oc[c, rows(d, st, band), :] = o[g]
            lc[c, rows(d, st, band), :] = jnp.broadcast_to(lse[g], (band, LANES))

    for blocks, with_prev in ((first_blocks, False), (later_blocks, True)):
        for i in range(0, len(blocks), LONGNET_GROUP):
            attend(blocks[i:i + LONGNET_GROUP], with_prev)

    lses = [lc[c] for c in range(len(DILATED_CONFIGS))]
    top = functools.reduce(jnp.maximum, lses)
    es = [jnp.exp(v - top) for v in lses]
    num = sum(e * oc[c] for c, e in enumerate(es))
    o_ref[0] = num / sum(es)


def _longnet(proj, cos, sin):
    b, s, _ = proj.shape
    n_cfg = len(DILATED_CONFIGS)
    blk = lambda off: pl.BlockSpec((1, s, HEAD_DIM), lambda i, h: (i, 0, off + h))
    tab = pl.BlockSpec((s, HEAD_DIM), lambda i, h: (0, 0))
    return pl.pallas_call(
        functools.partial(_longnet_kernel, seq=s),
        out_shape=jax.ShapeDtypeStruct((b, s, A_HEADS * HEAD_DIM), F32),
        grid=(b, A_HEADS),
        in_specs=[blk(0), blk(A_HEADS), blk(2 * A_HEADS), tab, tab],
        out_specs=pl.BlockSpec((1, s, HEAD_DIM), lambda i, h: (i, 0, h)),
        scratch_shapes=[pltpu.VMEM((s, HEAD_DIM), F32),
                        pltpu.VMEM((s, HEAD_DIM), F32),
                        pltpu.VMEM((s, HEAD_DIM), F32),
                        pltpu.VMEM((n_cfg, s, HEAD_DIM), F32),
                        pltpu.VMEM((n_cfg, s, LANES), F32)],
        compiler_params=_params("parallel", "parallel"),
        name="longnet_attention",
    )(proj, proj, proj, cos, sin)


def _softmax_blocks(qb, k_ref, v_ref, blocks, tile):
    scores = []
    m = None
    for start, mask in blocks:
        s = _dot_nt(qb, k_ref[pl.ds(start, tile), :])
        if mask is not None:
            s = jnp.where(mask, s, MASK_VALUE)
        scores.append(s)
        bm = jnp.max(s, axis=-1, keepdims=True)
        m = bm if m is None else jnp.maximum(m, bm)
    l = None
    acc = None
    for (start, _), s in zip(blocks, scores):
        p = jnp.exp(s - m)
        pl_sum = jnp.sum(p, axis=-1, keepdims=True)
        pv = _dot(p.astype(BF16), v_ref[pl.ds(start, tile), :])
        l = pl_sum if l is None else l + pl_sum
        acc = pv if acc is None else acc + pv
    return acc / l


def _moba_kernel(q_ref, k_ref, v_ref, cos_ref, sin_ref, o_ref, qs, ks, vs, *, seq):
    blk = MOBA_BLOCK
    nblk = seq // blk
    cos, sin = cos_ref[...], sin_ref[...]
    q = _rope(q_ref[0], cos, sin)
    k = _rope(k_ref[0], cos, sin)
    qs[...] = (q * (HEAD_DIM ** -0.5)).astype(BF16)
    ks[...] = k.astype(BF16)
    vs[...] = v_ref[0].astype(BF16)
    k_mean = jnp.concatenate(
        [jnp.mean(k[n * blk:(n + 1) * blk], axis=0, keepdims=True) for n in range(nblk)],
        axis=0)
    gate = _dot_nt_f32(q, k_mean)
    row = lax.broadcasted_iota(jnp.int32, (blk, blk), 0)
    col = lax.broadcasted_iota(jnp.int32, (blk, blk), 1)
    causal = col <= row
    n_top = min(MOBA_TOP_K, nblk)

    for i in range(nblk):
        g = gate[i * blk:(i + 1) * blk, :]
        blocks = [(i * blk, causal)]
        for n in range(i):
            if i <= n_top:
                blocks.append((n * blk, None))
                continue
            gn = g[:, n:n + 1]
            rank = jnp.zeros((blk, 1), jnp.int32)
            for o in range(i):
                if o == n:
                    continue
                go = g[:, o:o + 1]
                ahead = (go > gn) | (go == gn) if o < n else (go > gn)
                rank = rank + ahead.astype(jnp.int32)
            blocks.append((n * blk, rank < n_top))
        o_ref[0, pl.ds(i * blk, blk), :] = _softmax_blocks(
            qs[pl.ds(i * blk, blk), :], ks, vs, blocks, blk)


def _moba(proj, cos, sin):
    b, s, _ = proj.shape
    base = 3 * A_HEADS
    blk = lambda off: pl.BlockSpec((1, s, HEAD_DIM), lambda i, h: (i, 0, base + off + h))
    tab = pl.BlockSpec((s, HEAD_DIM), lambda i, h: (0, 0))
    return pl.pallas_call(
        functools.partial(_moba_kernel, seq=s),
        out_shape=jax.ShapeDtypeStruct((b, s, B_HEADS * HEAD_DIM), F32),
        grid=(b, B_HEADS),
        in_specs=[blk(0), blk(B_HEADS), blk(2 * B_HEADS), tab, tab],
        out_specs=pl.BlockSpec((1, s, HEAD_DIM), lambda i, h: (i, 0, h)),
        scratch_shapes=[pltpu.VMEM((s, HEAD_DIM), BF16)] * 3,
        compiler_params=_params("parallel", "parallel"),
        name="moba_attention",
    )(proj, proj, proj, cos, sin)


def _layer_norm(z, g, b):
    mu = jnp.mean(z, axis=-1, keepdims=True)
    zc = z - mu
    var = jnp.mean(zc * zc, axis=-1, keepdims=True)
    return zc * lax.rsqrt(var + LN_EPS) * g + b


def _route_topk(y, rw_ref, rb_ref, e_ref, g_ref):
    y_hi = y.astype(BF16)
    y_lo = (y - y_hi.astype(F32)).astype(BF16)
    hi_both = _dot(y_hi, rw_ref[...])
    lo_hi = _dot(y_lo, rw_ref[:, :LANES])
    logits = hi_both[:, :LANES] + (hi_both[:, LANES:] + lo_hi) + rb_ref[...]
    lane = lax.broadcasted_iota(jnp.int32, logits.shape, 1)
    vals, idxs = [], []
    for _ in range(MOE_TOP_K):
        v = jnp.max(logits, axis=-1, keepdims=True)
        ix = jnp.min(jnp.where(logits == v, lane, LANES), axis=-1, keepdims=True)
        vals.append(v)
        idxs.append(ix)
        logits = jnp.where(lane == ix, -jnp.inf, logits)
    exps = [jnp.exp(v - vals[0]) for v in vals]
    denom = sum(exps)
    e_out = jnp.zeros(lane.shape, jnp.int32)
    g_out = jnp.zeros(lane.shape, F32)
    for kk in range(MOE_TOP_K):
        e_out = jnp.where(lane == kk, idxs[kk], e_out)
        g_out = jnp.where(lane == kk, exps[kk] / denom, g_out)
    e_ref[...] = e_out
    g_ref[...] = g_out


_HI16 = 0xFFFF0000


def _pack_bf16_halves(y):
    half = y.shape[1] // 2
    bits = lambda v: lax.bitcast_convert_type(v.astype(BF16).astype(F32), jnp.uint32)
    return (bits(y[:, :half]) >> 16) | (bits(y[:, half:]) & jnp.uint32(_HI16))


def _unpack_bf16_halves(pk):
    lo = lax.bitcast_convert_type(pk << 16, F32).astype(BF16)
    hi = lax.bitcast_convert_type(pk & jnp.uint32(_HI16), F32).astype(BF16)
    return lo, hi


def _outproj_ln_kernel(*refs, n_in):
    o_refs, w_refs = refs[:n_in], refs[n_in:2 * n_in]
    x_ref, g_ref, b_ref, rw_ref, rb_ref, y_ref, pk_ref, e_ref, gt_ref = refs[2 * n_in:]
    h = sum(_dot(o[...].astype(BF16), w[...]) for o, w in zip(o_refs, w_refs))
    y = _layer_norm(ALPHA * x_ref[...] + h, g_ref[...], b_ref[...])
    y_ref[...] = y
    pk_ref[...] = _pack_bf16_halves(y)
    _route_topk(y, rw_ref, rb_ref, e_ref, gt_ref)


def _outproj_ln(outs, ws, x, g, b, rw, rb):
    t, d = x.shape
    tm = LN_TILE_M
    n_in = len(outs)
    row = lambda width: pl.BlockSpec((tm, width), lambda i: (i, 0))
    full = lambda a: pl.BlockSpec(a.shape, lambda i: (0,) * a.ndim)
    return pl.pallas_call(
        functools.partial(_outproj_ln_kernel, n_in=n_in),
        out_shape=(jax.ShapeDtypeStruct((t, d), F32),
                   jax.ShapeDtypeStruct((t, d // 2), jnp.uint32),
                   jax.ShapeDtypeStruct((t, LANES), jnp.int32),
                   jax.ShapeDtypeStruct((t, LANES), F32)),
        grid=(t // tm,),
        in_specs=([row(o.shape[1]) for o in outs] + [full(w) for w in ws]
                  + [row(d), full(g), full(b), full(rw), full(rb)]),
        out_specs=(row(d), row(d // 2), row(LANES), row(LANES)),
        compiler_params=_params("parallel"),
        name="outproj_layernorm_router",
    )(*outs, *ws, x, g, b, rw, rb)


def _rms_norm(x, g):
    return x * lax.rsqrt(jnp.mean(x * x, axis=-1, keepdims=True) + RMS_EPS) * g


def _mla_down_kernel(x_ref, w_ref, qn_ref, kvn_ref, cq_ref, ckv_ref, kr_ref):
    a = _dot(x_ref[...].astype(BF16), w_ref[...])
    cq_ref[...] = _rms_norm(a[:, :MLA_Q_RANK], qn_ref[...]).astype(BF16)
    ckv_ref[...] = _rms_norm(a[:, MLA_Q_RANK:MLA_Q_RANK + MLA_KV_RANK],
                             kvn_ref[...]).astype(BF16)
    kr_ref[...] = a[:, MLA_Q_RANK + MLA_KV_RANK:]


def _mla_down(x, w_cat, q_norm, kv_norm):
    t, d = x.shape
    tm = MM_TILE_M
    row = lambda width: pl.BlockSpec((tm, width), lambda i: (i, 0))
    full = lambda a: pl.BlockSpec(a.shape, lambda i: (0,) * a.ndim)
    return pl.pallas_call(
        _mla_down_kernel,
        out_shape=(jax.ShapeDtypeStruct((t, MLA_Q_RANK), BF16),
                   jax.ShapeDtypeStruct((t, MLA_KV_RANK), BF16),
                   jax.ShapeDtypeStruct((t, LANES), F32)),
        grid=(t // tm,),
        in_specs=[row(d), full(w_cat), full(q_norm), full(kv_norm)],
        out_specs=(row(MLA_Q_RANK), row(MLA_KV_RANK), row(LANES)),
        compiler_params=_params("parallel"),
        name="mla_down_rmsnorm",
    )(x, w_cat, q_norm, kv_norm)


def _mla_attn_kernel(cq_ref, ckv_ref, kr_ref, wq_ref, wkv_ref, cos_ref, sin_ref,
                     o_ref, qs, ks, vs, *, seq):
    tile = ATT_TILE_Q
    scale = (MLA_NOPE_DIM + MLA_ROPE_DIM) ** -0.5
    cos, sin = cos_ref[...], sin_ref[...]
    q = _dot(cq_ref[0], wq_ref[...])
    kv = _dot(ckv_ref[0], wkv_ref[...])
    qs[:, :LANES] = (q[:, :LANES] * scale).astype(BF16)
    qs[:, LANES:] = (_rope(q[:, LANES:], cos, sin) * scale).astype(BF16)
    ks[:, :LANES] = kv[:, :MLA_NOPE_DIM].astype(BF16)
    ks[:, LANES:] = _rope(kr_ref[0], cos, sin).astype(BF16)
    vs[...] = kv[:, MLA_NOPE_DIM:].astype(BF16)
    row = lax.broadcasted_iota(jnp.int32, (tile, tile), 0)
    col = lax.broadcasted_iota(jnp.int32, (tile, tile), 1)
    causal = col <= row
    for i in range(seq // tile):
        blocks = [(i * tile, causal)] + [(n * tile, None) for n in range(i)]
        o_ref[0, pl.ds(i * tile, tile), :] = _softmax_blocks(
            qs[pl.ds(i * tile, tile), :], ks, vs, blocks, tile)


def _mla_attn(cq, ckv, kr, wq, wkv, cos, sin):
    b, s, _ = cq.shape
    per_b = lambda width: pl.BlockSpec((1, s, width), lambda i, h: (i, 0, 0))
    head_w = lambda a: pl.BlockSpec((a.shape[0], 2 * LANES), lambda i, h: (0, h))
    tab = pl.BlockSpec((s, LANES), lambda i, h: (0, 0))
    return pl.pallas_call(
        functools.partial(_mla_attn_kernel, seq=s),
        out_shape=jax.ShapeDtypeStruct((b, s, MLA_HEADS * MLA_V_DIM), F32),
        grid=(b, MLA_HEADS),
        in_specs=[per_b(MLA_Q_RANK), per_b(MLA_KV_RANK), per_b(LANES),
                  head_w(wq), head_w(wkv), tab, tab],
        out_specs=pl.BlockSpec((1, s, MLA_V_DIM), lambda i, h: (i, 0, h)),
        scratch_shapes=[pltpu.VMEM((s, 2 * LANES), BF16),
                        pltpu.VMEM((s, 2 * LANES), BF16),
                        pltpu.VMEM((s, MLA_V_DIM), BF16)],
        compiler_params=_params("parallel", "arbitrary"),
        name="mla_attention",
    )(cq, ckv, kr, wq, wkv, cos, sin)


def _issue_row_gather(idx_of, src_hbm, n, sem_rows, dst_of, alternate=False):
    def issue(c, carry):
        for u in range(DMA_UNROLL):
            pltpu.make_async_copy(src_hbm.at[pl.ds(idx_of(c * DMA_UNROLL + u), 1)],
                                  dst_of(c, u), sem_rows).start(
                                      priority=u % 2 if alternate else 0)
        return carry
    lax.fori_loop(0, n // DMA_UNROLL, issue, 0)


def _for_row_counts(ns, tm, fn):
    for n in range(1, tm // MOE_SUB_M + 1):
        pl.when(ns == n)(functools.partial(fn, n * MOE_SUB_M))


def _moe_gu_kernel(te_ref, nu_ref, ns_ref, tok_hbm, x_hbm, wgu_ref, bgu_ref, cmp_ref,
                   o_ref, tok_smem, xbuf, xb, wgu_b, sem_idx, sem_rows):
    t, j = pl.program_id(0), pl.program_id(1)
    _, tm, half = xbuf.shape
    two_f = wgu_b.shape[1]
    sel = cmp_ref.shape[0]
    n_chunks = two_f // sel
    per_dot = tm // (pl.num_programs(1) * n_chunks)
    used = t < nu_ref[0]
    has_next = t + 1 < nu_ref[0]
    slot = t % 2

    def fetch_tokens(tile):
        cp = pltpu.make_async_copy(tok_hbm.at[tile], tok_smem, sem_idx)
        cp.start()
        cp.wait()

    def row_copy(sl, a):
        return pltpu.make_async_copy(x_hbm.at[pl.ds(tok_smem[a], 1)],
                                     xbuf.at[sl, pl.ds(a, 1)], sem_rows.at[sl])

    @pl.when(jnp.logical_and(used, j == 0))
    def _():
        @pl.when(t == 0)
        def _():
            fetch_tokens(0)
            _issue_row_gather(lambda a: tok_smem[a], x_hbm, tm, sem_rows.at[0],
                              lambda c, u: xbuf.at[0, pl.ds(c * DMA_UNROLL + u, 1)])

        pltpu.make_async_copy(x_hbm.at[pl.ds(0, tm)], xbuf.at[slot],
                              sem_rows.at[slot]).wait()

        def land(rows):
            r = pl.ds(0, rows)
            lo, hi = _unpack_bf16_halves(xbuf[slot, r, :])
            xb[r, :half] = lo
            xb[r, half:] = hi

        _for_row_counts(ns_ref[t], tm, land)

        @pl.when(has_next)
        def _():
            fetch_tokens(t + 1)

    @pl.when(jnp.logical_not(used))
    def _():
        o_ref[...] = jnp.zeros(o_ref.shape, o_ref.dtype)

    @pl.when(used)
    def _():
        wgu_b[...] = wgu_ref[0, 0].astype(BF16)

        def compute(prefetch, rows):
            r = pl.ds(0, rows)
            x = xb[r, :]
            is_glu = (lax.broadcasted_iota(jnp.int32, (rows, sel), 1) & 1) == 0
            for ci in range(n_chunks):
                c = ci * sel
                if prefetch:
                    for i in range(per_dot):
                        row_copy(1 - slot, (j * n_chunks + ci) * per_dot + i).start()
                h = _dot(x, wgu_b[:, c:c + sel]) + bgu_ref[0, 0, :, c:c + sel]
                glu = jnp.minimum(h, SWIGLU_LIMIT)
                lin = jnp.clip(h, -SWIGLU_LIMIT, SWIGLU_LIMIT) + 1.0
                lin = jnp.concatenate(
                    [pltpu.roll(lin[:, v:v + LANES], LANES - 1, 1)
                     for v in range(0, sel, LANES)], axis=1)
                act = jnp.where(is_glu, glu * jax.nn.sigmoid(SWIGLU_ALPHA * glu) * lin, 0.0)
                o_ref[r, c // 2:(c + sel) // 2] = _dot(
                    act.astype(BF16), cmp_ref[...]).astype(o_ref.dtype)
            if rows < tm:
                o_ref[pl.ds(rows, tm - rows), :] = jnp.zeros(
                    (tm - rows, o_ref.shape[1]), o_ref.dtype)

        pl.when(has_next)(
            lambda: _for_row_counts(ns_ref[t], tm, functools.partial(compute, True)))
        pl.when(jnp.logical_not(has_next))(
            lambda: _for_row_counts(ns_ref[t], tm, functools.partial(compute, False)))


def _moe_down_kernel(te_ref, nu_ref, ns_ref, a_ref, wd_ref, bd_ref, o_ref, wd_b):
    t = pl.program_id(0)
    tm = a_ref.shape[0]

    @pl.when(t >= nu_ref[0])
    def _():
        o_ref[...] = jnp.zeros(o_ref.shape, F32)

    @pl.when(t < nu_ref[0])
    def _():
        wd_b[...] = wd_ref[0, 0].astype(BF16)

        def compute(rows):
            r = pl.ds(0, rows)
            o_ref[r, :] = _dot(a_ref[r, :], wd_b[...]) + bd_ref[0, 0]
            if rows < tm:
                o_ref[pl.ds(rows, tm - rows), :] = jnp.zeros(
                    (tm - rows, o_ref.shape[1]), F32)

        _for_row_counts(ns_ref[t], tm, compute)


def _moe_experts(x_packed, tok_tiles, tile_expert, n_used, n_sub, w_gu, b_gu, w_down,
                 b_down, layer):
    n_tiles, tm = tok_tiles.shape
    half = x_packed.shape[1]
    d = 2 * half
    ff = w_down.shape[2]
    tf, tn = MOE_TILE_F, MOE_TILE_N
    n_f, n_n = ff // tf, d // tn
    sel = 4 * LANES
    compact = (jnp.arange(sel)[:, None] == 2 * jnp.arange(sel // 2)[None, :]).astype(BF16)

    def tile(i, nu):
        return jnp.minimum(i, nu[0] - 1)

    def chunk(i, j, nu, last):
        return jnp.where(i < nu[0], j, last)

    act = pl.pallas_call(
        _moe_gu_kernel,
        out_shape=jax.ShapeDtypeStruct((n_tiles * tm, ff), BF16),
        grid_spec=pltpu.PrefetchScalarGridSpec(
            num_scalar_prefetch=3,
            grid=(n_tiles, n_f),
            in_specs=[
                pl.BlockSpec(memory_space=pl.ANY),
                pl.BlockSpec(memory_space=pl.ANY),
                pl.BlockSpec((1, 1, d, 2 * tf), lambda i, j, te, nu, ns:
                             (layer, te[tile(i, nu)], 0, chunk(i, j, nu, n_f - 1))),
                pl.BlockSpec((1, 1, 1, 2 * tf), lambda i, j, te, nu, ns:
                             (layer, te[tile(i, nu)], 0, chunk(i, j, nu, n_f - 1))),
                pl.BlockSpec((sel, sel // 2), lambda i, j, te, nu, ns: (0, 0)),
            ],
            out_specs=pl.BlockSpec((tm, tf), lambda i, j, te, nu, ns: (i, j)),
            scratch_shapes=[pltpu.SMEM((tm,), jnp.int32),
                            pltpu.VMEM((2, tm, half), jnp.uint32),
                            pltpu.VMEM((tm, d), BF16),
                            pltpu.VMEM((d, 2 * tf), BF16),
                            pltpu.SemaphoreType.DMA,
                            pltpu.SemaphoreType.DMA((2,))]),
        compiler_params=_params("arbitrary", "arbitrary"),
        name="moe_gate_up",
    )(tile_expert, n_used, n_sub, tok_tiles, x_packed, w_gu, b_gu[:, :, None, :], compact)

    return pl.pallas_call(
        _moe_down_kernel,
        out_shape=jax.ShapeDtypeStruct((n_tiles * tm, d), F32),
        grid_spec=pltpu.PrefetchScalarGridSpec(
            num_scalar_prefetch=3,
            grid=(n_tiles, n_n),
            in_specs=[
                pl.BlockSpec((tm, ff), lambda i, j, te, nu, ns: (tile(i, nu), 0)),
                pl.BlockSpec((1, 1, ff, tn), lambda i, j, te, nu, ns:
                             (layer, te[tile(i, nu)], 0, chunk(i, j, nu, n_n - 1))),
                pl.BlockSpec((1, 1, 1, tn), lambda i, j, te, nu, ns:
                             (layer, te[tile(i, nu)], 0, chunk(i, j, nu, n_n - 1))),
            ],
            out_specs=pl.BlockSpec((tm, tn), lambda i, j, te, nu, ns: (i, j)),
            scratch_shapes=[pltpu.VMEM((ff, tn), BF16)]),
        compiler_params=_params("arbitrary", "arbitrary"),
        name="moe_down",
    )(tile_expert, n_used, n_sub, act, w_down, b_down[:, :, None, :])


def _combine_kernel(pos_hbm, ys_hbm, gates_ref, x_ref, g_ref, b_ref, y_ref,
                    pos_smem, buf, sem_idx, sem_rows):
    i = pl.program_id(0)
    n = pl.num_programs(0)
    _, k, tc, _ = buf.shape
    per_slot = tc // DMA_UNROLL
    slot = i % 2

    def idx_copy(tile, sl):
        return pltpu.make_async_copy(pos_hbm.at[tile], pos_smem.at[sl], sem_idx.at[sl])

    def start_gather(sl):
        _issue_row_gather(
            lambda a: pos_smem[sl, a], ys_hbm, k * tc, sem_rows.at[sl],
            lambda c, u: buf.at[sl, c // per_slot,
                                pl.ds((c % per_slot) * DMA_UNROLL + u, 1)],
            alternate=True)

    @pl.when(i == 0)
    def _():
        idx_copy(0, 0).start()
        idx_copy(0, 0).wait()
        start_gather(0)

        @pl.when(n > 1)
        def _():
            idx_copy(1, 1).start()

    for kk in range(k):
        pltpu.make_async_copy(ys_hbm.at[pl.ds(0, tc)], buf.at[slot, kk],
                              sem_rows.at[slot]).wait()

    @pl.when(i + 1 < n)
    def _():
        idx_copy(i + 1, 1 - slot).wait()
        start_gather(1 - slot)

        @pl.when(i + 2 < n)
        def _():
            idx_copy(i + 2, slot).start()

    gates = gates_ref[...]
    h = sum(gates[:, kk:kk + 1] * buf[slot, kk] for kk in range(k))
    y_ref[...] = _layer_norm(ALPHA * x_ref[...] + h, g_ref[...], b_ref[...])


def _combine_ln(pos_tiles, ys, gates, x, g, b):
    t, d = x.shape
    tc = COMBINE_TILE
    row = lambda width: pl.BlockSpec((tc, width), lambda i: (i, 0))
    full = lambda a: pl.BlockSpec(a.shape, lambda i: (0,) * a.ndim)
    anyspec = pl.BlockSpec(memory_space=pl.ANY)
    return pl.pallas_call(
        _combine_kernel,
        out_shape=jax.ShapeDtypeStruct((t, d), F32),
        grid=(t // tc,),
        in_specs=[anyspec, anyspec, row(LANES), row(d), full(g), full(b)],
        out_specs=row(d),
        scratch_shapes=[pltpu.SMEM((2, MOE_TOP_K * tc), jnp.int32),
                        pltpu.VMEM((2, MOE_TOP_K, tc, d), F32),
                        pltpu.SemaphoreType.DMA((2,)),
                        pltpu.SemaphoreType.DMA((2,))],
        compiler_params=_params("arbitrary"),
        name="moe_combine_layernorm",
    )(pos_tiles, ys, gates, x, g, b)


def _rope_tables(seq, dim, slab):
    half = dim // 2
    pos = jnp.arange(seq, dtype=F32)
    inv = 1.0 / (ROPE_THETA ** (jnp.arange(0, dim, 2, dtype=F32) / dim))
    ang = pos[:, None] * inv[None, :]
    pad = jnp.zeros((seq, slab // 2 - half), F32)
    cos = jnp.concatenate([jnp.cos(ang), pad, jnp.cos(ang), pad], axis=1)
    sin = jnp.concatenate([-jnp.sin(ang), pad, jnp.sin(ang), pad], axis=1)
    return cos, sin


def _pad_rope_cols(w):
    half = MLA_ROPE_DIM // 2
    z = jnp.zeros(w.shape[:-1] + (LANES // 2 - half,), w.dtype)
    return jnp.concatenate([w[..., :half], z, w[..., half:], z], axis=-1)


def _route(top_e, tm, sub):
    t = top_e.shape[0]
    i32 = jnp.int32
    flat_e = top_e.reshape(-1)
    n_assign = t * MOE_TOP_K
    n_tiles = n_assign // tm + N_EXPERTS
    order = jnp.argsort(flat_e).astype(i32)
    sorted_e = flat_e[order]
    counts = jnp.sum(flat_e[:, None] == jnp.arange(N_EXPERTS, dtype=i32)[None, :],
                     axis=0, dtype=i32)
    padded = (counts + tm - 1) // tm * tm
    start = jnp.cumsum(counts) - counts
    pad_end = jnp.cumsum(padded)
    pad_start = pad_end - padded
    tile_start = jnp.arange(n_tiles, dtype=i32) * tm
    tile_expert = jnp.minimum(
        jnp.sum(pad_end[None, :] <= tile_start[:, None], axis=1, dtype=i32), N_EXPERTS - 1)
    n_used = (pad_end[-1:] // tm).astype(i32)
    first = tile_start - pad_start[tile_expert]
    tile_rows = jnp.clip(counts[tile_expert] - first, 0, tm)
    n_sub = (tile_rows + sub - 1) // sub
    k_in_e = first[:, None] + jnp.arange(tm, dtype=i32)[None, :]
    valid = k_in_e < counts[tile_expert][:, None]
    src = jnp.clip(start[tile_expert][:, None] + k_in_e, 0, n_assign - 1)
    row_token = jnp.where(valid, order[src] // MOE_TOP_K, 0).astype(i32)
    dest = (pad_start[sorted_e] + jnp.arange(n_assign, dtype=i32) - start[sorted_e]).astype(i32)
    _, pos = lax.sort_key_val(order, dest)
    return row_token, pos, tile_expert, n_used, n_sub.astype(i32)


def _moe_layer(x_packed, top_e, w_gu, b_gu, w_down, b_down, layer):
    tok_tiles, pos, tile_expert, n_used, n_sub = _route(top_e, MOE_TILE_M, MOE_SUB_M)
    ys = _moe_experts(x_packed, tok_tiles, tile_expert, n_used, n_sub,
                      w_gu, b_gu, w_down, b_down, layer)
    pos_tiles = pos.reshape(-1, COMBINE_TILE, MOE_TOP_K).transpose(0, 2, 1)
    return ys, pos_tiles.reshape(-1, MOE_TOP_K * COMBINE_TILE)


def _router_params(router_w, router_b):
    rw = jnp.pad(router_w, ((0, 0), (0, LANES - N_EXPERTS)))
    rw_hi = lax.reduce_precision(rw, exponent_bits=8, mantissa_bits=7)
    rw_lo = rw - rw_hi
    rb = jnp.pad(router_b, (0, LANES - N_EXPERTS), constant_values=MASK_VALUE)
    return jnp.concatenate([rw_hi, rw_lo], axis=1).astype(BF16), rb[None, :]


def kernel(x, ln_mix_g, ln_mix_b, ln_ffn_g, ln_ffn_b, w_in_ab, w_out_ab,
           mla_w_dq, mla_q_norm, mla_w_uq, mla_w_dkv, mla_kv_norm, mla_w_ukv, mla_w_out,
           router_w, router_b, moe_w_gu, moe_b_gu, moe_w_down, moe_b_down):
    b, s, d = x.shape
    t = b * s
    xt = x.reshape(t, d)
    cos_h, sin_h = _rope_tables(s, HEAD_DIM, LANES)
    cos_r, sin_r = _rope_tables(s, MLA_ROPE_DIM, LANES)
    row2 = lambda v: v[None, :]

    for layer in range(DEPTH):
        i = layer // 2
        rw, rb = _router_params(router_w[layer], router_b[layer])
        if layer % 2 == 0:
            proj = _matmul(xt, w_in_ab[i].astype(BF16)).reshape(b, s, -1)
            o_a = _longnet(proj, cos_h, sin_h).reshape(t, -1)
            o_b = _moba(proj, cos_h, sin_h).reshape(t, -1)
            w_out = w_out_ab[i].astype(BF16)
            split = A_HEADS * HEAD_DIM
            outs, ws = [o_a, o_b], [w_out[:split], w_out[split:]]
        else:
            w_cat = jnp.concatenate(
                [mla_w_dq[i], mla_w_dkv[i][:, :MLA_KV_RANK],
                 _pad_rope_cols(mla_w_dkv[i][:, MLA_KV_RANK:])], axis=1).astype(BF16)
            cq, ckv, kr = _mla_down(xt, w_cat, row2(mla_q_norm[i]), row2(mla_kv_norm[i]))
            wq = mla_w_uq[i].reshape(MLA_Q_RANK, MLA_HEADS, MLA_NOPE_DIM + MLA_ROPE_DIM)
            wq = jnp.concatenate([wq[..., :MLA_NOPE_DIM],
                                  _pad_rope_cols(wq[..., MLA_NOPE_DIM:])], axis=-1)
            wq = wq.reshape(MLA_Q_RANK, MLA_HEADS * 2 * LANES).astype(BF16)
            o = _mla_attn(cq.reshape(b, s, -1), ckv.reshape(b, s, -1), kr.reshape(b, s, -1),
                          wq, mla_w_ukv[i].astype(BF16), cos_r, sin_r)
            outs, ws = [o.reshape(t, -1)], [mla_w_out[i].astype(BF16)]
        xt, x_packed, top_e, gates = _outproj_ln(outs, ws, xt, row2(ln_mix_g[layer]),
                                                 row2(ln_mix_b[layer]), rw, rb)
        ys, pos_tiles = _moe_layer(x_packed, top_e[:, :MOE_TOP_K], moe_w_gu, moe_b_gu,
                                   moe_w_down, moe_b_down, layer)
        xt = _combine_ln(pos_tiles, ys, gates, xt,
                         row2(ln_ffn_g[layer]), row2(ln_ffn_b[layer]))
    return xt.reshape(b, s, d)
```

```python
import functools

import jax
import jax.numpy as jnp
from jax import lax
from jax.experimental import pallas as pl
from jax.experimental.pallas import tpu as pltpu

F32 = jnp.float32
BF16 = jnp.bfloat16

DEPTH = 2
HEAD_DIM = 128
A_HEADS = 12
B_HEADS = 4
DILATED_CONFIGS = ((128, 1), (512, 4), (2048, 16))
MOBA_BLOCK = 256
MOBA_TOP_K = 3
ROPE_THETA = 10000.0
MLA_HEADS = 16
MLA_Q_RANK = 512
MLA_KV_RANK = 512
MLA_NOPE_DIM = 128
MLA_ROPE_DIM = 64
MLA_V_DIM = 128
N_EXPERTS = 32
MOE_TOP_K = 4
SWIGLU_LIMIT = 7.0
SWIGLU_ALPHA = 1.702
LN_EPS = 1e-5
RMS_EPS = 1e-6
ALPHA = (2.0 * DEPTH) ** 0.25

LANES = 128
MASK_VALUE = -1e30
VMEM_LIMIT = 56 * 1024 * 1024

MM_TILE_M = 1024
MM_TILE_N = 1024
LN_TILE_M = 256
ATT_TILE_Q = 256
LONGNET_GROUP = 4
MOE_TILE_M = 1024
MOE_SUB_M = 256
MOE_TILE_F = 512
MOE_TILE_N = 1024
COMBINE_TILE = 256
DMA_UNROLL = 8


def _params(*semantics):
    return pltpu.CompilerParams(dimension_semantics=semantics,
                                vmem_limit_bytes=VMEM_LIMIT)


def _dot(a, b):
    return jnp.dot(a, b, preferred_element_type=F32)


def _dot_nt(a, b):
    return lax.dot_general(a, b, (((1,), (1,)), ((), ())),
                           preferred_element_type=F32)


def _dot_nt_f32(a, b):
    return lax.dot_general(a, b, (((1,), (1,)), ((), ())),
                           precision=lax.Precision.HIGHEST,
                           preferred_element_type=F32)


def _rope(x, cos, sin):
    return x * cos + pltpu.roll(x, LANES // 2, 1) * sin


def _mm_kernel(x_ref, w_ref, o_ref, xb_ref):
    @pl.when(pl.program_id(1) == 0)
    def _():
        xb_ref[...] = x_ref[...].astype(BF16)
    o_ref[...] = _dot(xb_ref[...], w_ref[...])


def _matmul(x, w):
    m, k = x.shape
    n = w.shape[1]
    tm, tn = min(MM_TILE_M, m), min(MM_TILE_N, n)
    return pl.pallas_call(
        _mm_kernel,
        out_shape=jax.ShapeDtypeStruct((m, n), F32),
        grid=(m // tm, n // tn),
        in_specs=[pl.BlockSpec((tm, k), lambda i, j: (i, 0)),
                  pl.BlockSpec((k, tn), lambda i, j: (0, j))],
        out_specs=pl.BlockSpec((tm, tn), lambda i, j: (i, j)),
        scratch_shapes=[pltpu.VMEM((tm, k), BF16)],
        compiler_params=_params("parallel", "arbitrary"),
        name="proj_matmul",
    )(x, w)


def _longnet_kernel(q_ref, k_ref, v_ref, cos_ref, sin_ref, o_ref,
                    qs, ks, vs, oc, lc, *, seq):
    band = DILATED_CONFIGS[0][0] // DILATED_CONFIGS[0][1]
    cos, sin = cos_ref[...], sin_ref[...]
    qs[...] = _rope(q_ref[0], cos, sin) * (HEAD_DIM ** -0.5)
    ks[...] = _rope(k_ref[0], cos, sin)
    vs[...] = v_ref[0]

    def band_mask(g, with_prev):
        width = 2 * band if with_prev else band
        qi = lax.broadcasted_iota(jnp.int32, (g, band, width), 1)
        kj = lax.broadcasted_iota(jnp.int32, (g, band, width), 2)
        if with_prev:
            return jnp.logical_and(kj >= qi, kj <= qi + band)
        return kj <= qi

    def rows(dil, start, size):
        return pl.ds(start, size) if dil == 1 else pl.ds(start, size, stride=dil)

    first_blocks, later_blocks = [], []
    for c, (window, dil) in enumerate(DILATED_CONFIGS):
        assert window // dil == band and seq % (dil * band) == 0
        for r in range(dil):
            for n in range(seq // (dil * band)):
                (later_blocks if n else first_blocks).append((c, dil, r + dil * band * n))

    def attend(group, with_prev):
        def stacked(ref, back, size):
            return jnp.stack([ref[rows(d, st - back * d * band, size), :]
                              for _, d, st in group]).astype(BF16)
        qb = stacked(qs, 0, band)
        kb = stacked(ks, 1, 2 * band) if with_prev else stacked(ks, 0, band)
        vb = stacked(vs, 1, 2 * band) if with_prev else stacked(vs, 0, band)
        s = lax.dot_general(qb, kb, (((2,), (2,)), ((0,), (0,))),
                            preferred_element_type=F32)
        s = jnp.where(band_mask(len(group), with_prev), s, MASK_VALUE)
        m = jnp.max(s, axis=-1, keepdims=True)
        p = jnp.exp(s - m)
        l = jnp.sum(p, axis=-1, keepdims=True)
        acc = lax.dot_general(p.astype(BF16), vb, (((2,), (1,)), ((0,), (0,))),
                              preferred_element_type=F32)
        o = acc / l
        lse = m + jnp.log(l)
        for g, (c, d, st) in enumerate(group):
            oc[c, rows(d, st, band), :] = o[g]
            lc[c, rows(d, st, band), :] = jnp.broadcast_to(lse[g], (band, LANES))

    for blocks, with_prev in ((first_blocks, False), (later_blocks, True)):
        for i in range(0, len(blocks), LONGNET_GROUP):
            attend(blocks[i:i + LONGNET_GROUP], with_prev)

    lses = [lc[c] for c in range(len(DILATED_CONFIGS))]
    top = functools.reduce(jnp.maximum, lses)
    es = [jnp.exp(v - top) for v in lses]
    num = sum(e * oc[c] for c, e in enumerate(es))
    o_ref[0] = num / sum(es)


def _longnet(proj, cos, sin):
    b, s, _ = proj.shape
    n_cfg = len(DILATED_CONFIGS)
    blk = lambda off: pl.BlockSpec((1, s, HEAD_DIM), lambda i, h: (i, 0, off + h))
    tab = pl.BlockSpec((s, HEAD_DIM), lambda i, h: (0, 0))
    return pl.pallas_call(
        functools.partial(_longnet_kernel, seq=s),
        out_shape=jax.ShapeDtypeStruct((b, s, A_HEADS * HEAD_DIM), F32),
        grid=(b, A_HEADS),
        in_specs=[blk(0), blk(A_HEADS), blk(2 * A_HEADS), tab, tab],
        out_specs=pl.BlockSpec((1, s, HEAD_DIM), lambda i, h: (i, 0, h)),
        scratch_shapes=[pltpu.VMEM((s, HEAD_DIM), F32),
                        pltpu.VMEM((s, HEAD_DIM), F32),
                        pltpu.VMEM((s, HEAD_DIM), F32),
                        pltpu.VMEM((n_cfg, s, HEAD_DIM), F32),
                        pltpu.VMEM((n_cfg, s, LANES), F32)],
        compiler_params=_params("parallel", "parallel"),
        name="longnet_attention",
    )(proj, proj, proj, cos, sin)


def _softmax_blocks(qb, k_ref, v_ref, blocks, tile):
    scores = []
    m = None
    for start, mask in blocks:
        s = _dot_nt(qb, k_ref[pl.ds(start, tile), :])
        if mask is not None:
            s = jnp.where(mask, s, MASK_VALUE)
        scores.append(s)
        bm = jnp.max(s, axis=-1, keepdims=True)
        m = bm if m is None else jnp.maximum(m, bm)
    l = None
    acc = None
    for (start, _), s in zip(blocks, scores):
        p = jnp.exp(s - m)
        pl_sum = jnp.sum(p, axis=-1, keepdims=True)
        pv = _dot(p.astype(BF16), v_ref[pl.ds(start, tile), :])
        l = pl_sum if l is None else l + pl_sum
        acc = pv if acc is None else acc + pv
    return acc / l


def _moba_kernel(q_ref, k_ref, v_ref, cos_ref, sin_ref, o_ref, qs, ks, vs, *, seq):
    blk = MOBA_BLOCK
    nblk = seq // blk
    cos, sin = cos_ref[...], sin_ref[...]
    q = _rope(q_ref[0], cos, sin)
    k = _rope(k_ref[0], cos, sin)
    qs[...] = (q * (HEAD_DIM ** -0.5)).astype(BF16)
    ks[...] = k.astype(BF16)
    vs[...] = v_ref[0].astype(BF16)
    k_mean = jnp.concatenate(
        [jnp.mean(k[n * blk:(n + 1) * blk], axis=0, keepdims=True) for n in range(nblk)],
        axis=0)
    gate = _dot_nt_f32(q, k_mean)
    row = lax.broadcasted_iota(jnp.int32, (blk, blk), 0)
    col = lax.broadcasted_iota(jnp.int32, (blk, blk), 1)
    causal = col <= row
    n_top = min(MOBA_TOP_K, nblk)

    for i in range(nblk):
        g = gate[i * blk:(i + 1) * blk, :]
        blocks = [(i * blk, causal)]
        for n in range(i):
            if i <= n_top:
                blocks.append((n * blk, None))
                continue
            gn = g[:, n:n + 1]
            rank = jnp.zeros((blk, 1), jnp.int32)
            for o in range(i):
                if o == n:
                    continue
                go = g[:, o:o + 1]
                ahead = (go > gn) | (go == gn) if o < n else (go > gn)
                rank = rank + ahead.astype(jnp.int32)
            blocks.append((n * blk, rank < n_top))
        o_ref[0, pl.ds(i * blk, blk), :] = _softmax_blocks(
            qs[pl.ds(i * blk, blk), :], ks, vs, blocks, blk)


def _moba(proj, cos, sin):
    b, s, _ = proj.shape
    base = 3 * A_HEADS
    blk = lambda off: pl.BlockSpec((1, s, HEAD_DIM), lambda i, h: (i, 0, base + off + h))
    tab = pl.BlockSpec((s, HEAD_DIM), lambda i, h: (0, 0))
    return pl.pallas_call(
        functools.partial(_moba_kernel, seq=s),
        out_shape=jax.ShapeDtypeStruct((b, s, B_HEADS * HEAD_DIM), F32),
        grid=(b, B_HEADS),
        in_specs=[blk(0), blk(B_HEADS), blk(2 * B_HEADS), tab, tab],
        out_specs=pl.BlockSpec((1, s, HEAD_DIM), lambda i, h: (i, 0, h)),
        scratch_shapes=[pltpu.VMEM((s, HEAD_DIM), BF16)] * 3,
        compiler_params=_params("parallel", "parallel"),
        name="moba_attention",
    )(proj, proj, proj, cos, sin)


def _layer_norm(z, g, b):
    mu = jnp.mean(z, axis=-1, keepdims=True)
    zc = z - mu
    var = jnp.mean(zc * zc, axis=-1, keepdims=True)
    return zc * lax.rsqrt(var + LN_EPS) * g + b


def _route_topk(y, rw_ref, rb_ref, e_ref, g_ref):
    y_hi = y.astype(BF16)
    y_lo = (y - y_hi.astype(F32)).astype(BF16)
    hi_both = _dot(y_hi, rw_ref[...])
    lo_hi = _dot(y_lo, rw_ref[:, :LANES])
    logits = hi_both[:, :LANES] + (hi_both[:, LANES:] + lo_hi) + rb_ref[...]
    lane = lax.broadcasted_iota(jnp.int32, logits.shape, 1)
    vals, idxs = [], []
    for _ in range(MOE_TOP_K):
        v = jnp.max(logits, axis=-1, keepdims=True)
        ix = jnp.min(jnp.where(logits == v, lane, LANES), axis=-1, keepdims=True)
        vals.append(v)
        idxs.append(ix)
        logits = jnp.where(lane == ix, -jnp.inf, logits)
    exps = [jnp.exp(v - vals[0]) for v in vals]
    denom = sum(exps)
    e_out = jnp.zeros(lane.shape, jnp.int32)
    g_out = jnp.zeros(lane.shape, F32)
    for kk in range(MOE_TOP_K):
        e_out = jnp.where(lane == kk, idxs[kk], e_out)
        g_out = jnp.where(lane == kk, exps[kk] / denom, g_out)
    e_ref[...] = e_out
    g_ref[...] = g_out


_HI16 = 0xFFFF0000


def _pack_bf16_halves(y):
    half = y.shape[1] // 2
    bits = lambda v: lax.bitcast_convert_type(v.astype(BF16).astype(F32), jnp.uint32)
    return (bits(y[:, :half]) >> 16) | (bits(y[:, half:]) & jnp.uint32(_HI16))


def _unpack_bf16_halves(pk):
    lo = lax.bitcast_convert_type(pk << 16, F32).astype(BF16)
    hi = lax.bitcast_convert_type(pk & jnp.uint32(_HI16), F32).astype(BF16)
    return lo, hi


def _outproj_ln_kernel(*refs, n_in):
    o_refs, w_refs = refs[:n_in], refs[n_in:2 * n_in]
    x_ref, g_ref, b_ref, rw_ref, rb_ref, y_ref, pk_ref, e_ref, gt_ref = refs[2 * n_in:]
    h = sum(_dot(o[...].astype(BF16), w[...]) for o, w in zip(o_refs, w_refs))
    y = _layer_norm(ALPHA * x_ref[...] + h, g_ref[...], b_ref[...])
    y_ref[...] = y
    pk_ref[...] = _pack_bf16_halves(y)
    _route_topk(y, rw_ref, rb_ref, e_ref, gt_ref)


def _outproj_ln(outs, ws, x, g, b, rw, rb):
    t, d = x.shape
    tm = LN_TILE_M
    n_in = len(outs)
    row = lambda width: pl.BlockSpec((tm, width), lambda i: (i, 0))
    full = lambda a: pl.BlockSpec(a.shape, lambda i: (0,) * a.ndim)
    return pl.pallas_call(
        functools.partial(_outproj_ln_kernel, n_in=n_in),
        out_shape=(jax.ShapeDtypeStruct((t, d), F32),
                   jax.ShapeDtypeStruct((t, d // 2), jnp.uint32),
                   jax.ShapeDtypeStruct((t, LANES), jnp.int32),
                   jax.ShapeDtypeStruct((t, LANES), F32)),
        grid=(t // tm,),
        in_specs=([row(o.shape[1]) for o in outs] + [full(w) for w in ws]
                  + [row(d), full(g), full(b), full(rw), full(rb)]),
        out_specs=(row(d), row(d // 2), row(LANES), row(LANES)),
        compiler_params=_params("parallel"),
        name="outproj_layernorm_router",
    )(*outs, *ws, x, g, b, rw, rb)


def _rms_norm(x, g):
    return x * lax.rsqrt(jnp.mean(x * x, axis=-1, keepdims=True) + RMS_EPS) * g


def _mla_down_kernel(x_ref, w_ref, qn_ref, kvn_ref, cq_ref, ckv_ref, kr_ref):
    a = _dot(x_ref[...].astype(BF16), w_ref[...])
    cq_ref[...] = _rms_norm(a[:, :MLA_Q_RANK], qn_ref[...]).astype(BF16)
    ckv_ref[...] = _rms_norm(a[:, MLA_Q_RANK:MLA_Q_RANK + MLA_KV_RANK],
                             kvn_ref[...]).astype(BF16)
    kr_ref[...] = a[:, MLA_Q_RANK + MLA_KV_RANK:]


def _mla_down(x, w_cat, q_norm, kv_norm):
    t, d = x.shape
    tm = MM_TILE_M
    row = lambda width: pl.BlockSpec((tm, width), lambda i: (i, 0))
    full = lambda a: pl.BlockSpec(a.shape, lambda i: (0,) * a.ndim)
    return pl.pallas_call(
        _mla_down_kernel,
        out_shape=(jax.ShapeDtypeStruct((t, MLA_Q_RANK), BF16),
                   jax.ShapeDtypeStruct((t, MLA_KV_RANK), BF16),
                   jax.ShapeDtypeStruct((t, LANES), F32)),
        grid=(t // tm,),
        in_specs=[row(d), full(w_cat), full(q_norm), full(kv_norm)],
        out_specs=(row(MLA_Q_RANK), row(MLA_KV_RANK), row(LANES)),
        compiler_params=_params("parallel"),
        name="mla_down_rmsnorm",
    )(x, w_cat, q_norm, kv_norm)


def _mla_attn_kernel(cq_ref, ckv_ref, kr_ref, wq_ref, wkv_ref, cos_ref, sin_ref,
                     o_ref, qs, ks, vs, *, seq):
    tile = ATT_TILE_Q
    scale = (MLA_NOPE_DIM + MLA_ROPE_DIM) ** -0.5
    cos, sin = cos_ref[...], sin_ref[...]
    q = _dot(cq_ref[0], wq_ref[...])
    kv = _dot(ckv_ref[0], wkv_ref[...])
    qs[:, :LANES] = (q[:, :LANES] * scale).astype(BF16)
    qs[:, LANES:] = (_rope(q[:, LANES:], cos, sin) * scale).astype(BF16)
    ks[:, :LANES] = kv[:, :MLA_NOPE_DIM].astype(BF16)
    ks[:, LANES:] = _rope(kr_ref[0], cos, sin).astype(BF16)
    vs[...] = kv[:, MLA_NOPE_DIM:].astype(BF16)
    row = lax.broadcasted_iota(jnp.int32, (tile, tile), 0)
    col = lax.broadcasted_iota(jnp.int32, (tile, tile), 1)
    causal = col <= row
    for i in range(seq // tile):
        past = i * tile
        keys = pl.ds(0, past + tile)
        s = _dot_nt(qs[pl.ds(past, tile), :], ks[keys, :])
        s_diag = jnp.where(causal, s[:, past:], MASK_VALUE)
        m = jnp.max(s_diag, axis=-1, keepdims=True)
        if i:
            m = jnp.maximum(m, jnp.max(s[:, :past], axis=-1, keepdims=True))
            p = jnp.concatenate([jnp.exp(s[:, :past] - m), jnp.exp(s_diag - m)], axis=1)
        else:
            p = jnp.exp(s_diag - m)
        l = jnp.sum(p, axis=-1, keepdims=True)
        o_ref[0, pl.ds(past, tile), :] = _dot(p.astype(BF16), vs[keys, :]) / l


def _mla_attn(cq, ckv, kr, wq, wkv, cos, sin):
    b, s, _ = cq.shape
    per_b = lambda width: pl.BlockSpec((1, s, width), lambda i, h: (i, 0, 0))
    head_w = lambda a: pl.BlockSpec((a.shape[0], 2 * LANES), lambda i, h: (0, h))
    tab = pl.BlockSpec((s, LANES), lambda i, h: (0, 0))
    return pl.pallas_call(
        functools.partial(_mla_attn_kernel, seq=s),
        out_shape=jax.ShapeDtypeStruct((b, s, MLA_HEADS * MLA_V_DIM), F32),
        grid=(b, MLA_HEADS),
        in_specs=[per_b(MLA_Q_RANK), per_b(MLA_KV_RANK), per_b(LANES),
                  head_w(wq), head_w(wkv), tab, tab],
        out_specs=pl.BlockSpec((1, s, MLA_V_DIM), lambda i, h: (i, 0, h)),
        scratch_shapes=[pltpu.VMEM((s, 2 * LANES), BF16),
                        pltpu.VMEM((s, 2 * LANES), BF16),
                        pltpu.VMEM((s, MLA_V_DIM), BF16)],
        compiler_params=_params("parallel", "arbitrary"),
        name="mla_attention",
    )(cq, ckv, kr, wq, wkv, cos, sin)


def _issue_row_gather(idx_of, src_hbm, n, sem_rows, dst_of, alternate=False):
    def issue(c, carry):
        for u in range(DMA_UNROLL):
            pltpu.make_async_copy(src_hbm.at[pl.ds(idx_of(c * DMA_UNROLL + u), 1)],
                                  dst_of(c, u), sem_rows).start(
                                      priority=u % 2 if alternate else 0)
        return carry
    lax.fori_loop(0, n // DMA_UNROLL, issue, 0)


def _for_row_counts(ns, tm, fn):
    for n in range(1, tm // MOE_SUB_M + 1):
        pl.when(ns == n)(functools.partial(fn, n * MOE_SUB_M))


def _moe_gu_kernel(te_ref, nu_ref, ns_ref, tok_hbm, x_hbm, wgu_ref, bgu_ref, cmp_ref,
                   o_ref, tok_smem, xbuf, xb, wgu_b, sem_idx, sem_rows):
    t, j = pl.program_id(0), pl.program_id(1)
    _, tm, half = xbuf.shape
    two_f = wgu_b.shape[1]
    sel = cmp_ref.shape[0]
    used = t < nu_ref[0]
    slot = t % 2

    def idx_copy(tile, sl):
        return pltpu.make_async_copy(tok_hbm.at[tile], tok_smem.at[sl], sem_idx.at[sl])

    def start_gather(tile, sl):
        _issue_row_gather(lambda a: tok_smem[sl, a], x_hbm, ns_ref[tile] * MOE_SUB_M,
                          sem_rows.at[sl],
                          lambda c, u: xbuf.at[sl, pl.ds(c * DMA_UNROLL + u, 1)])

    @pl.when(jnp.logical_and(used, j == 0))
    def _():
        @pl.when(t == 0)
        def _():
            idx_copy(0, 0).start()
            idx_copy(0, 0).wait()
            start_gather(0, 0)

            @pl.when(nu_ref[0] > 1)
            def _():
                idx_copy(1, 1).start()

        def land(rows):
            r = pl.ds(0, rows)
            pltpu.make_async_copy(x_hbm.at[r], xbuf.at[slot, r], sem_rows.at[slot]).wait()
            lo, hi = _unpack_bf16_halves(xbuf[slot, r, :])
            xb[r, :half] = lo
            xb[r, half:] = hi

        _for_row_counts(ns_ref[t], tm, land)

        @pl.when(t + 1 < nu_ref[0])
        def _():
            idx_copy(t + 1, 1 - slot).wait()
            start_gather(t + 1, 1 - slot)

            @pl.when(t + 2 < nu_ref[0])
            def _():
                idx_copy(t + 2, slot).start()

    @pl.when(jnp.logical_not(used))
    def _():
        o_ref[...] = jnp.zeros(o_ref.shape, o_ref.dtype)

    @pl.when(used)
    def _():
        wgu_b[...] = wgu_ref[0, 0].astype(BF16)

        def compute(rows):
            r = pl.ds(0, rows)
            x = xb[r, :]
            is_glu = (lax.broadcasted_iota(jnp.int32, (rows, sel), 1) & 1) == 0
            for c in range(0, two_f, sel):
                h = _dot(x, wgu_b[:, c:c + sel]) + bgu_ref[0, 0, :, c:c + sel]
                glu = jnp.minimum(h, SWIGLU_LIMIT)
                lin = jnp.clip(h, -SWIGLU_LIMIT, SWIGLU_LIMIT) + 1.0
                lin = jnp.concatenate(
                    [pltpu.roll(lin[:, v:v + LANES], LANES - 1, 1)
                     for v in range(0, sel, LANES)], axis=1)
                act = jnp.where(is_glu, glu * jax.nn.sigmoid(SWIGLU_ALPHA * glu) * lin, 0.0)
                o_ref[r, c // 2:(c + sel) // 2] = _dot(
                    act.astype(BF16), cmp_ref[...]).astype(o_ref.dtype)
            if rows < tm:
                o_ref[pl.ds(rows, tm - rows), :] = jnp.zeros(
                    (tm - rows, o_ref.shape[1]), o_ref.dtype)

        _for_row_counts(ns_ref[t], tm, compute)


def _moe_down_kernel(te_ref, nu_ref, ns_ref, a_ref, wd_ref, bd_ref, o_ref, wd_b):
    t = pl.program_id(1)
    tm = a_ref.shape[0]

    @pl.when(t >= nu_ref[0])
    def _():
        o_ref[...] = jnp.zeros(o_ref.shape, F32)

    @pl.when(t < nu_ref[0])
    def _():
        @pl.when(jnp.logical_or(t == 0, te_ref[t] != te_ref[jnp.maximum(t - 1, 0)]))
        def _():
            wd_b[...] = wd_ref[0, 0].astype(BF16)

        def compute(rows):
            r = pl.ds(0, rows)
            o_ref[r, :] = _dot(a_ref[r, :], wd_b[...]) + bd_ref[0, 0]
            if rows < tm:
                o_ref[pl.ds(rows, tm - rows), :] = jnp.zeros(
                    (tm - rows, o_ref.shape[1]), F32)

        _for_row_counts(ns_ref[t], tm, compute)


def _moe_experts(x_packed, tok_tiles, tile_expert, n_used, n_sub, w_gu, b_gu, w_down,
                 b_down, layer):
    n_tiles, tm = tok_tiles.shape
    half = x_packed.shape[1]
    d = 2 * half
    ff = w_down.shape[2]
    tf, tn = MOE_TILE_F, MOE_TILE_N
    n_f, n_n = ff // tf, d // tn
    sel = 4 * LANES
    compact = (jnp.arange(sel)[:, None] == 2 * jnp.arange(sel // 2)[None, :]).astype(BF16)

    def tile(i, nu):
        return jnp.minimum(i, nu[0] - 1)

    def chunk(i, j, nu, last):
        return jnp.where(i < nu[0], j, last)

    act = pl.pallas_call(
        _moe_gu_kernel,
        out_shape=jax.ShapeDtypeStruct((n_tiles * tm, ff), BF16),
        grid_spec=pltpu.PrefetchScalarGridSpec(
            num_scalar_prefetch=3,
            grid=(n_tiles, n_f),
            in_specs=[
                pl.BlockSpec(memory_space=pl.ANY),
                pl.BlockSpec(memory_space=pl.ANY),
                pl.BlockSpec((1, 1, d, 2 * tf), lambda i, j, te, nu, ns:
                             (layer, te[tile(i, nu)], 0, chunk(i, j, nu, n_f - 1))),
                pl.BlockSpec((1, 1, 1, 2 * tf), lambda i, j, te, nu, ns:
                             (layer, te[tile(i, nu)], 0, chunk(i, j, nu, n_f - 1))),
                pl.BlockSpec((sel, sel // 2), lambda i, j, te, nu, ns: (0, 0)),
            ],
            out_specs=pl.BlockSpec((tm, tf), lambda i, j, te, nu, ns: (i, j)),
            scratch_shapes=[pltpu.SMEM((2, tm), jnp.int32),
                            pltpu.VMEM((2, tm, half), jnp.uint32),
                            pltpu.VMEM((tm, d), BF16),
                            pltpu.VMEM((d, 2 * tf), BF16),
                            pltpu.SemaphoreType.DMA((2,)),
                            pltpu.SemaphoreType.DMA((2,))]),
        compiler_params=_params("arbitrary", "arbitrary"),
        name="moe_gate_up",
    )(tile_expert, n_used, n_sub, tok_tiles, x_packed, w_gu, b_gu[:, :, None, :], compact)

    return pl.pallas_call(
        _moe_down_kernel,
        out_shape=jax.ShapeDtypeStruct((n_tiles * tm, d), F32),
        grid_spec=pltpu.PrefetchScalarGridSpec(
            num_scalar_prefetch=3,
            grid=(n_n, n_tiles),
            in_specs=[
                pl.BlockSpec((tm, ff), lambda j, i, te, nu, ns: (tile(i, nu), 0)),
                pl.BlockSpec((1, 1, ff, tn), lambda j, i, te, nu, ns:
                             (layer, te[tile(i, nu)], 0, j)),
                pl.BlockSpec((1, 1, 1, tn), lambda j, i, te, nu, ns:
                             (layer, te[tile(i, nu)], 0, j)),
            ],
            out_specs=pl.BlockSpec((tm, tn), lambda j, i, te, nu, ns: (i, j)),
            scratch_shapes=[pltpu.VMEM((ff, tn), BF16)]),
        compiler_params=_params("arbitrary", "arbitrary"),
        name="moe_down",
    )(tile_expert, n_used, n_sub, act, w_down, b_down[:, :, None, :])


def _combine_kernel(pos_hbm, ys_hbm, gates_ref, x_ref, g_ref, b_ref, y_ref,
                    pos_smem, buf, sem_idx, sem_rows):
    i = pl.program_id(0)
    n = pl.num_programs(0)
    _, k, tc, _ = buf.shape
    per_slot = tc // DMA_UNROLL
    slot = i % 2

    def idx_copy(tile, sl):
        return pltpu.make_async_copy(pos_hbm.at[tile], pos_smem.at[sl], sem_idx.at[sl])

    def start_gather(sl):
        _issue_row_gather(
            lambda a: pos_smem[sl, a], ys_hbm, k * tc, sem_rows.at[sl],
            lambda c, u: buf.at[sl, c // per_slot,
                                pl.ds((c % per_slot) * DMA_UNROLL + u, 1)],
            alternate=True)

    @pl.when(i == 0)
    def _():
        idx_copy(0, 0).start()
        idx_copy(0, 0).wait()
        start_gather(0)

        @pl.when(n > 1)
        def _():
            idx_copy(1, 1).start()

    for kk in range(k):
        pltpu.make_async_copy(ys_hbm.at[pl.ds(0, tc)], buf.at[slot, kk],
                              sem_rows.at[slot]).wait()

    @pl.when(i + 1 < n)
    def _():
        idx_copy(i + 1, 1 - slot).wait()
        start_gather(1 - slot)

        @pl.when(i + 2 < n)
        def _():
            idx_copy(i + 2, slot).start()

    gates = gates_ref[...]
    h = sum(gates[:, kk:kk + 1] * buf[slot, kk] for kk in range(k))
    y_ref[...] = _layer_norm(ALPHA * x_ref[...] + h, g_ref[...], b_ref[...])


def _combine_ln(pos_tiles, ys, gates, x, g, b):
    t, d = x.shape
    tc = COMBINE_TILE
    row = lambda width: pl.BlockSpec((tc, width), lambda i: (i, 0))
    full = lambda a: pl.BlockSpec(a.shape, lambda i: (0,) * a.ndim)
    anyspec = pl.BlockSpec(memory_space=pl.ANY)
    return pl.pallas_call(
        _combine_kernel,
        out_shape=jax.ShapeDtypeStruct((t, d), F32),
        grid=(t // tc,),
        in_specs=[anyspec, anyspec, row(LANES), row(d), full(g), full(b)],
        out_specs=row(d),
        scratch_shapes=[pltpu.SMEM((2, MOE_TOP_K * tc), jnp.int32),
                        pltpu.VMEM((2, MOE_TOP_K, tc, d), F32),
                        pltpu.SemaphoreType.DMA((2,)),
                        pltpu.SemaphoreType.DMA((2,))],
        compiler_params=_params("arbitrary"),
        name="moe_combine_layernorm",
    )(pos_tiles, ys, gates, x, g, b)


def _rope_tables(seq, dim, slab):
    half = dim // 2
    pos = jnp.arange(seq, dtype=F32)
    inv = 1.0 / (ROPE_THETA ** (jnp.arange(0, dim, 2, dtype=F32) / dim))
    ang = pos[:, None] * inv[None, :]
    pad = jnp.zeros((seq, slab // 2 - half), F32)
    cos = jnp.concatenate([jnp.cos(ang), pad, jnp.cos(ang), pad], axis=1)
    sin = jnp.concatenate([-jnp.sin(ang), pad, jnp.sin(ang), pad], axis=1)
    return cos, sin


def _pad_rope_cols(w):
    half = MLA_ROPE_DIM // 2
    z = jnp.zeros(w.shape[:-1] + (LANES // 2 - half,), w.dtype)
    return jnp.concatenate([w[..., :half], z, w[..., half:], z], axis=-1)


def _route(top_e, tm, sub):
    t = top_e.shape[0]
    i32 = jnp.int32
    flat_e = top_e.reshape(-1)
    n_assign = t * MOE_TOP_K
    n_tiles = n_assign // tm + N_EXPERTS
    order = jnp.argsort(flat_e).astype(i32)
    sorted_e = flat_e[order]
    counts = jnp.sum(flat_e[:, None] == jnp.arange(N_EXPERTS, dtype=i32)[None, :],
                     axis=0, dtype=i32)
    padded = (counts + tm - 1) // tm * tm
    start = jnp.cumsum(counts) - counts
    pad_end = jnp.cumsum(padded)
    pad_start = pad_end - padded
    tile_start = jnp.arange(n_tiles, dtype=i32) * tm
    tile_expert = jnp.minimum(
        jnp.sum(pad_end[None, :] <= tile_start[:, None], axis=1, dtype=i32), N_EXPERTS - 1)
    n_used = (pad_end[-1:] // tm).astype(i32)
    first = tile_start - pad_start[tile_expert]
    tile_rows = jnp.clip(counts[tile_expert] - first, 0, tm)
    n_sub = (tile_rows + sub - 1) // sub
    k_in_e = first[:, None] + jnp.arange(tm, dtype=i32)[None, :]
    valid = k_in_e < counts[tile_expert][:, None]
    src = jnp.clip(start[tile_expert][:, None] + k_in_e, 0, n_assign - 1)
    row_token = jnp.where(valid, order[src] // MOE_TOP_K, 0).astype(i32)
    dest = (pad_start[sorted_e] + jnp.arange(n_assign, dtype=i32) - start[sorted_e]).astype(i32)
    _, pos = lax.sort_key_val(order, dest)
    return row_token, pos, tile_expert, n_used, n_sub.astype(i32)


def _moe_layer(x_packed, top_e, w_gu, b_gu, w_down, b_down, layer):
    tok_tiles, pos, tile_expert, n_used, n_sub = _route(top_e, MOE_TILE_M, MOE_SUB_M)
    ys = _moe_experts(x_packed, tok_tiles, tile_expert, n_used, n_sub,
                      w_gu, b_gu, w_down, b_down, layer)
    pos_tiles = pos.reshape(-1, COMBINE_TILE, MOE_TOP_K).transpose(0, 2, 1)
    return ys, pos_tiles.reshape(-1, MOE_TOP_K * COMBINE_TILE)


def _router_params(router_w, router_b):
    rw = jnp.pad(router_w, ((0, 0), (0, LANES - N_EXPERTS)))
    rw_hi = lax.reduce_precision(rw, exponent_bits=8, mantissa_bits=7)
    rw_lo = rw - rw_hi
    rb = jnp.pad(router_b, (0, LANES - N_EXPERTS), constant_values=MASK_VALUE)
    return jnp.concatenate([rw_hi, rw_lo], axis=1).astype(BF16), rb[None, :]


def kernel(x, ln_mix_g, ln_mix_b, ln_ffn_g, ln_ffn_b, w_in_ab, w_out_ab,
           mla_w_dq, mla_q_norm, mla_w_uq, mla_w_dkv, mla_kv_norm, mla_w_ukv, mla_w_out,
           router_w, router_b, moe_w_gu, moe_b_gu, moe_w_down, moe_b_down):
    b, s, d = x.shape
    t = b * s
    xt = x.reshape(t, d)
    cos_h, sin_h = _rope_tables(s, HEAD_DIM, LANES)
    cos_r, sin_r = _rope_tables(s, MLA_ROPE_DIM, LANES)
    row2 = lambda v: v[None, :]

    for layer in range(DEPTH):
        i = layer // 2
        rw, rb = _router_params(router_w[layer], router_b[layer])
        if layer % 2 == 0:
            proj = _matmul(xt, w_in_ab[i].astype(BF16)).reshape(b, s, -1)
            o_a = _longnet(proj, cos_h, sin_h).reshape(t, -1)
            o_b = _moba(proj, cos_h, sin_h).reshape(t, -1)
            w_out = w_out_ab[i].astype(BF16)
            split = A_HEADS * HEAD_DIM
            outs, ws = [o_a, o_b], [w_out[:split], w_out[split:]]
        else:
            w_cat = jnp.concatenate(
                [mla_w_dq[i], mla_w_dkv[i][:, :MLA_KV_RANK],
                 _pad_rope_cols(mla_w_dkv[i][:, MLA_KV_RANK:])], axis=1).astype(BF16)
            cq, ckv, kr = _mla_down(xt, w_cat, row2(mla_q_norm[i]), row2(mla_kv_norm[i]))
            wq = mla_w_uq[i].reshape(MLA_Q_RANK, MLA_HEADS, MLA_NOPE_DIM + MLA_ROPE_DIM)
            wq = jnp.concatenate([wq[..., :MLA_NOPE_DIM],
                                  _pad_rope_cols(wq[..., MLA_NOPE_DIM:])], axis=-1)
            wq = wq.reshape(MLA_Q_RANK, MLA_HEADS * 2 * LANES).astype(BF16)
            o = _mla_attn(cq.reshape(b, s, -1), ckv.reshape(b, s, -1), kr.reshape(b, s, -1),
                          wq, mla_w_ukv[i].astype(BF16), cos_r, sin_r)
            outs, ws = [o.reshape(t, -1)], [mla_w_out[i].astype(BF16)]
        xt, x_packed, top_e, gates = _outproj_ln(outs, ws, xt, row2(ln_mix_g[layer]),
                                                 row2(ln_mix_b[layer]), rw, rb)
        ys, pos_tiles = _moe_layer(x_packed, top_e[:, :MOE_TOP_K], moe_w_gu, moe_b_gu,
                                   moe_w_down, moe_b_down, layer)
        xt = _combine_ln(pos_tiles, ys, gates, xt,
                         row2(ln_ffn_g[layer]), row2(ln_ffn_b[layer]))
    return xt.reshape(b, s, d)
```

```python
import functools

import jax
import jax.numpy as jnp
from jax import lax
from jax.experimental import pallas as pl
from jax.experimental.pallas import tpu as pltpu

F32 = jnp.float32
BF16 = jnp.bfloat16

DEPTH = 2
HEAD_DIM = 128
A_HEADS = 12
B_HEADS = 4
DILATED_CONFIGS = ((128, 1), (512, 4), (2048, 16))
MOBA_BLOCK = 256
MOBA_TOP_K = 3
ROPE_THETA = 10000.0
MLA_HEADS = 16
MLA_Q_RANK = 512
MLA_KV_RANK = 512
MLA_NOPE_DIM = 128
MLA_ROPE_DIM = 64
MLA_V_DIM = 128
N_EXPERTS = 32
MOE_TOP_K = 4
SWIGLU_LIMIT = 7.0
SWIGLU_ALPHA = 1.702
LN_EPS = 1e-5
RMS_EPS = 1e-6
ALPHA = (2.0 * DEPTH) ** 0.25

LANES = 128
MASK_VALUE = -1e30
VMEM_LIMIT = 56 * 1024 * 1024

MM_TILE_M = 1024
MM_TILE_N = 1024
LN_TILE_M = 256
ATT_TILE_Q = 256
LONGNET_GROUP = 8
MOE_TILE_M = 1024
MOE_SUB_M = 256
MOE_TILE_F = 512
MOE_TILE_N = 1024
COMBINE_TILE = 256
DMA_UNROLL = 16


def _params(*semantics):
    return pltpu.CompilerParams(dimension_semantics=semantics,
                                vmem_limit_bytes=VMEM_LIMIT)


def _dot(a, b):
    return jnp.dot(a, b, preferred_element_type=F32)


def _dot_nt(a, b):
    return lax.dot_general(a, b, (((1,), (1,)), ((), ())),
                           preferred_element_type=F32)


def _dot_nt_f32(a, b):
    return lax.dot_general(a, b, (((1,), (1,)), ((), ())),
                           precision=lax.Precision.HIGHEST,
                           preferred_element_type=F32)


def _exp_bf16(z):
    return jnp.exp(z.astype(BF16))


def _with_ones(v):
    return jnp.concatenate([v, jnp.ones(v.shape, v.dtype)], axis=-1)


def _rope(x, cos, sin):
    return x * cos + pltpu.roll(x, LANES // 2, 1) * sin


def _mm_kernel(x_ref, w_ref, o_ref, xb_ref):
    @pl.when(pl.program_id(1) == 0)
    def _():
        xb_ref[...] = x_ref[...].astype(BF16)
    o_ref[...] = _dot(xb_ref[...], w_ref[...])


def _matmul(x, w):
    m, k = x.shape
    n = w.shape[1]
    tm, tn = min(MM_TILE_M, m), min(MM_TILE_N, n)
    return pl.pallas_call(
        _mm_kernel,
        out_shape=jax.ShapeDtypeStruct((m, n), F32),
        grid=(m // tm, n // tn),
        in_specs=[pl.BlockSpec((tm, k), lambda i, j: (i, 0)),
                  pl.BlockSpec((k, tn), lambda i, j: (0, j))],
        out_specs=pl.BlockSpec((tm, tn), lambda i, j: (i, j)),
        scratch_shapes=[pltpu.VMEM((tm, k), BF16)],
        compiler_params=_params("parallel", "arbitrary"),
        name="proj_matmul",
    )(x, w)


def _longnet_kernel(q_ref, k_ref, v_ref, cos_ref, sin_ref, o_ref,
                    qs, ks, vs, oc, lc, *, seq):
    band = DILATED_CONFIGS[0][0] // DILATED_CONFIGS[0][1]
    cos, sin = cos_ref[...], sin_ref[...]
    qs[...] = _rope(q_ref[0], cos, sin) * (HEAD_DIM ** -0.5)
    ks[...] = _rope(k_ref[0], cos, sin)
    vs[...] = v_ref[0]

    def band_mask(g, with_prev):
        width = 2 * band if with_prev else band
        qi = lax.broadcasted_iota(jnp.int32, (g, band, width), 1)
        kj = lax.broadcasted_iota(jnp.int32, (g, band, width), 2)
        if with_prev:
            return jnp.logical_and(kj >= qi, kj <= qi + band)
        return kj <= qi

    def rows(dil, start, size):
        return pl.ds(start, size) if dil == 1 else pl.ds(start, size, stride=dil)

    first_blocks, later_blocks = [], []
    for c, (window, dil) in enumerate(DILATED_CONFIGS):
        assert window // dil == band and seq % (dil * band) == 0
        for r in range(dil):
            for n in range(seq // (dil * band)):
                (later_blocks if n else first_blocks).append((c, dil, r + dil * band * n))

    def attend(group, with_prev):
        def stacked(ref, back, size):
            return jnp.stack([ref[rows(d, st - back * d * band, size), :]
                              for _, d, st in group]).astype(BF16)
        qb = stacked(qs, 0, band)
        kb = stacked(ks, 1, 2 * band) if with_prev else stacked(ks, 0, band)
        vb = stacked(vs, 1, 2 * band) if with_prev else stacked(vs, 0, band)
        s = lax.dot_general(qb, kb, (((2,), (2,)), ((0,), (0,))),
                            preferred_element_type=F32)
        s = jnp.where(band_mask(len(group), with_prev), s, MASK_VALUE)
        m = jnp.max(s, axis=-1, keepdims=True)
        acc = lax.dot_general(_exp_bf16(s - m), _with_ones(vb), (((2,), (1,)), ((0,), (0,))),
                              preferred_element_type=F32)
        l = acc[:, :, HEAD_DIM:]
        o = acc[:, :, :HEAD_DIM] / l
        lse = m + jnp.log(l[:, :, :1])
        for g, (c, d, st) in enumerate(group):
            oc[c, rows(d, st, band), :] = o[g]
            lc[c, rows(d, st, band), :] = jnp.broadcast_to(lse[g], (band, LANES))

    for blocks, with_prev in ((first_blocks, False), (later_blocks, True)):
        for i in range(0, len(blocks), LONGNET_GROUP):
            attend(blocks[i:i + LONGNET_GROUP], with_prev)

    lses = [lc[c] for c in range(len(DILATED_CONFIGS))]
    top = functools.reduce(jnp.maximum, lses)
    es = [jnp.exp(v - top) for v in lses]
    num = sum(e * oc[c] for c, e in enumerate(es))
    o_ref[0] = num / sum(es)


def _longnet(proj, cos, sin):
    b, s, _ = proj.shape
    n_cfg = len(DILATED_CONFIGS)
    blk = lambda off: pl.BlockSpec((1, s, HEAD_DIM), lambda i, h: (i, 0, off + h))
    tab = pl.BlockSpec((s, HEAD_DIM), lambda i, h: (0, 0))
    return pl.pallas_call(
        functools.partial(_longnet_kernel, seq=s),
        out_shape=jax.ShapeDtypeStruct((b, s, A_HEADS * HEAD_DIM), F32),
        grid=(b, A_HEADS),
        in_specs=[blk(0), blk(A_HEADS), blk(2 * A_HEADS), tab, tab],
        out_specs=pl.BlockSpec((1, s, HEAD_DIM), lambda i, h: (i, 0, h)),
        scratch_shapes=[pltpu.VMEM((s, HEAD_DIM), F32),
                        pltpu.VMEM((s, HEAD_DIM), F32),
                        pltpu.VMEM((s, HEAD_DIM), F32),
                        pltpu.VMEM((n_cfg, s, HEAD_DIM), F32),
                        pltpu.VMEM((n_cfg, s, LANES), F32)],
        compiler_params=_params("parallel", "parallel"),
        name="longnet_attention",
    )(proj, proj, proj, cos, sin)


def _softmax_blocks(qb, k_ref, v1_ref, blocks, tile):
    scores = []
    m = None
    for start, mask in blocks:
        s = _dot_nt(qb, k_ref[pl.ds(start, tile), :])
        if mask is not None:
            s = jnp.where(mask, s, MASK_VALUE)
        scores.append(s)
        bm = jnp.max(s, axis=-1, keepdims=True)
        m = bm if m is None else jnp.maximum(m, bm)
    acc = None
    for (start, _), s in zip(blocks, scores):
        pv = _dot(_exp_bf16(s - m), v1_ref[pl.ds(start, tile), :])
        acc = pv if acc is None else acc + pv
    half = acc.shape[1] // 2
    return acc[:, :half] / acc[:, half:]


def _moba_kernel(q_ref, k_ref, v_ref, cos_ref, sin_ref, o_ref, qs, ks, vs, *, seq):
    blk = MOBA_BLOCK
    nblk = seq // blk
    cos, sin = cos_ref[...], sin_ref[...]
    q = _rope(q_ref[0], cos, sin)
    k = _rope(k_ref[0], cos, sin)
    qs[...] = (q * (HEAD_DIM ** -0.5)).astype(BF16)
    ks[...] = k.astype(BF16)
    vs[...] = _with_ones(v_ref[0].astype(BF16))
    k_mean = jnp.concatenate(
        [jnp.mean(k[n * blk:(n + 1) * blk], axis=0, keepdims=True) for n in range(nblk)],
        axis=0)
    gate = _dot_nt_f32(q, k_mean)
    row = lax.broadcasted_iota(jnp.int32, (blk, blk), 0)
    col = lax.broadcasted_iota(jnp.int32, (blk, blk), 1)
    causal = col <= row
    n_top = min(MOBA_TOP_K, nblk)

    for i in range(nblk):
        g = gate[i * blk:(i + 1) * blk, :]
        blocks = [(i * blk, causal)]
        for n in range(i):
            if i <= n_top:
                blocks.append((n * blk, None))
                continue
            gn = g[:, n:n + 1]
            rank = jnp.zeros((blk, 1), jnp.int32)
            for o in range(i):
                if o == n:
                    continue
                go = g[:, o:o + 1]
                ahead = (go > gn) | (go == gn) if o < n else (go > gn)
                rank = rank + ahead.astype(jnp.int32)
            blocks.append((n * blk, rank < n_top))
        o_ref[0, pl.ds(i * blk, blk), :] = _softmax_blocks(
            qs[pl.ds(i * blk, blk), :], ks, vs, blocks, blk)


def _moba(proj, cos, sin):
    b, s, _ = proj.shape
    base = 3 * A_HEADS
    blk = lambda off: pl.BlockSpec((1, s, HEAD_DIM), lambda i, h: (i, 0, base + off + h))
    tab = pl.BlockSpec((s, HEAD_DIM), lambda i, h: (0, 0))
    return pl.pallas_call(
        functools.partial(_moba_kernel, seq=s),
        out_shape=jax.ShapeDtypeStruct((b, s, B_HEADS * HEAD_DIM), F32),
        grid=(b, B_HEADS),
        in_specs=[blk(0), blk(B_HEADS), blk(2 * B_HEADS), tab, tab],
        out_specs=pl.BlockSpec((1, s, HEAD_DIM), lambda i, h: (i, 0, h)),
        scratch_shapes=[pltpu.VMEM((s, HEAD_DIM), BF16)] * 2
        + [pltpu.VMEM((s, 2 * HEAD_DIM), BF16)],
        compiler_params=_params("parallel", "parallel"),
        name="moba_attention",
    )(proj, proj, proj, cos, sin)


def _layer_norm(z, g, b):
    mu = jnp.mean(z, axis=-1, keepdims=True)
    zc = z - mu
    var = jnp.mean(zc * zc, axis=-1, keepdims=True)
    return zc * lax.rsqrt(var + LN_EPS) * g + b


def _route_topk(y, rw_ref, rb_ref, e_ref, g_ref):
    y_hi = y.astype(BF16)
    y_lo = (y - y_hi.astype(F32)).astype(BF16)
    hi_both = _dot(y_hi, rw_ref[...])
    lo_hi = _dot(y_lo, rw_ref[:, :LANES])
    logits = hi_both[:, :LANES] + (hi_both[:, LANES:] + lo_hi) + rb_ref[...]
    lane = lax.broadcasted_iota(jnp.int32, logits.shape, 1)
    vals, idxs = [], []
    for _ in range(MOE_TOP_K):
        v = jnp.max(logits, axis=-1, keepdims=True)
        ix = jnp.min(jnp.where(logits == v, lane, LANES), axis=-1, keepdims=True)
        vals.append(v)
        idxs.append(ix)
        logits = jnp.where(lane == ix, -jnp.inf, logits)
    exps = [jnp.exp(v - vals[0]) for v in vals]
    denom = sum(exps)
    e_out = jnp.zeros(lane.shape, jnp.int32)
    g_out = jnp.zeros(lane.shape, F32)
    for kk in range(MOE_TOP_K):
        e_out = jnp.where(lane == kk, idxs[kk], e_out)
        g_out = jnp.where(lane == kk, exps[kk] / denom, g_out)
    e_ref[...] = e_out
    g_ref[...] = g_out


_HI16 = 0xFFFF0000


def _pack_bf16_halves(y):
    half = y.shape[1] // 2
    bits = lambda v: lax.bitcast_convert_type(v.astype(BF16).astype(F32), jnp.uint32)
    return (bits(y[:, :half]) >> 16) | (bits(y[:, half:]) & jnp.uint32(_HI16))


def _unpack_bf16_halves(pk):
    lo = lax.bitcast_convert_type(pk << 16, F32).astype(BF16)
    hi = lax.bitcast_convert_type(pk & jnp.uint32(_HI16), F32).astype(BF16)
    return lo, hi


def _outproj_ln_kernel(*refs, n_in):
    o_refs, w_refs = refs[:n_in], refs[n_in:2 * n_in]
    x_ref, g_ref, b_ref, rw_ref, rb_ref, y_ref, pk_ref, e_ref, gt_ref = refs[2 * n_in:]
    h = sum(_dot(o[...].astype(BF16), w[...]) for o, w in zip(o_refs, w_refs))
    y = _layer_norm(ALPHA * x_ref[...] + h, g_ref[...], b_ref[...])
    y_ref[...] = y
    pk_ref[...] = _pack_bf16_halves(y)
    _route_topk(y, rw_ref, rb_ref, e_ref, gt_ref)


def _outproj_ln(outs, ws, x, g, b, rw, rb):
    t, d = x.shape
    tm = LN_TILE_M
    n_in = len(outs)
    row = lambda width: pl.BlockSpec((tm, width), lambda i: (i, 0))
    full = lambda a: pl.BlockSpec(a.shape, lambda i: (0,) * a.ndim)
    return pl.pallas_call(
        functools.partial(_outproj_ln_kernel, n_in=n_in),
        out_shape=(jax.ShapeDtypeStruct((t, d), F32),
                   jax.ShapeDtypeStruct((t, d // 2), jnp.uint32),
                   jax.ShapeDtypeStruct((t, LANES), jnp.int32),
                   jax.ShapeDtypeStruct((t, LANES), F32)),
        grid=(t // tm,),
        in_specs=([row(o.shape[1]) for o in outs] + [full(w) for w in ws]
                  + [row(d), full(g), full(b), full(rw), full(rb)]),
        out_specs=(row(d), row(d // 2), row(LANES), row(LANES)),
        compiler_params=_params("parallel"),
        name="outproj_layernorm_router",
    )(*outs, *ws, x, g, b, rw, rb)


def _rms_norm(x, g):
    return x * lax.rsqrt(jnp.mean(x * x, axis=-1, keepdims=True) + RMS_EPS) * g


def _mla_down_kernel(x_ref, w_ref, qn_ref, kvn_ref, cq_ref, ckv_ref, kr_ref):
    a = _dot(x_ref[...].astype(BF16), w_ref[...])
    cq_ref[...] = _rms_norm(a[:, :MLA_Q_RANK], qn_ref[...]).astype(BF16)
    ckv_ref[...] = _rms_norm(a[:, MLA_Q_RANK:MLA_Q_RANK + MLA_KV_RANK],
                             kvn_ref[...]).astype(BF16)
    kr_ref[...] = a[:, MLA_Q_RANK + MLA_KV_RANK:]


def _mla_down(x, w_cat, q_norm, kv_norm):
    t, d = x.shape
    tm = MM_TILE_M
    row = lambda width: pl.BlockSpec((tm, width), lambda i: (i, 0))
    full = lambda a: pl.BlockSpec(a.shape, lambda i: (0,) * a.ndim)
    return pl.pallas_call(
        _mla_down_kernel,
        out_shape=(jax.ShapeDtypeStruct((t, MLA_Q_RANK), BF16),
                   jax.ShapeDtypeStruct((t, MLA_KV_RANK), BF16),
                   jax.ShapeDtypeStruct((t, LANES), F32)),
        grid=(t // tm,),
        in_specs=[row(d), full(w_cat), full(q_norm), full(kv_norm)],
        out_specs=(row(MLA_Q_RANK), row(MLA_KV_RANK), row(LANES)),
        compiler_params=_params("parallel"),
        name="mla_down_rmsnorm",
    )(x, w_cat, q_norm, kv_norm)


def _mla_attn_kernel(cq_ref, ckv_ref, kr_ref, wq_ref, wkv_ref, cos_ref, sin_ref,
                     o_ref, qs, ks, vs, *, seq):
    tile = ATT_TILE_Q
    scale = (MLA_NOPE_DIM + MLA_ROPE_DIM) ** -0.5
    cos, sin = cos_ref[...], sin_ref[...]
    q = _dot(cq_ref[0], wq_ref[...])
    kv = _dot(ckv_ref[0], wkv_ref[...])
    qs[:, :LANES] = (q[:, :LANES] * scale).astype(BF16)
    qs[:, LANES:] = (_rope(q[:, LANES:], cos, sin) * scale).astype(BF16)
    ks[:, :LANES] = kv[:, :MLA_NOPE_DIM].astype(BF16)
    ks[:, LANES:] = _rope(kr_ref[0], cos, sin).astype(BF16)
    vs[...] = _with_ones(kv[:, MLA_NOPE_DIM:].astype(BF16))
    row = lax.broadcasted_iota(jnp.int32, (tile, tile), 0)
    col = lax.broadcasted_iota(jnp.int32, (tile, tile), 1)
    causal = col <= row
    for i in range(seq // tile):
        past = i * tile
        keys = pl.ds(0, past + tile)
        s = _dot_nt(qs[pl.ds(past, tile), :], ks[keys, :])
        s_diag = jnp.where(causal, s[:, past:], MASK_VALUE)
        m = jnp.max(s_diag, axis=-1, keepdims=True)
        if i:
            m = jnp.maximum(m, jnp.max(s[:, :past], axis=-1, keepdims=True))
            p = jnp.concatenate([_exp_bf16(s[:, :past] - m), _exp_bf16(s_diag - m)], axis=1)
        else:
            p = _exp_bf16(s_diag - m)
        acc = _dot(p, vs[keys, :])
        o_ref[0, pl.ds(past, tile), :] = acc[:, :MLA_V_DIM] / acc[:, MLA_V_DIM:]


def _mla_attn(cq, ckv, kr, wq, wkv, cos, sin):
    b, s, _ = cq.shape
    per_b = lambda width: pl.BlockSpec((1, s, width), lambda i, h: (i, 0, 0))
    head_w = lambda a: pl.BlockSpec((a.shape[0], 2 * LANES), lambda i, h: (0, h))
    tab = pl.BlockSpec((s, LANES), lambda i, h: (0, 0))
    return pl.pallas_call(
        functools.partial(_mla_attn_kernel, seq=s),
        out_shape=jax.ShapeDtypeStruct((b, s, MLA_HEADS * MLA_V_DIM), F32),
        grid=(b, MLA_HEADS),
        in_specs=[per_b(MLA_Q_RANK), per_b(MLA_KV_RANK), per_b(LANES),
                  head_w(wq), head_w(wkv), tab, tab],
        out_specs=pl.BlockSpec((1, s, MLA_V_DIM), lambda i, h: (i, 0, h)),
        scratch_shapes=[pltpu.VMEM((s, 2 * LANES), BF16),
                        pltpu.VMEM((s, 2 * LANES), BF16),
                        pltpu.VMEM((s, 2 * MLA_V_DIM), BF16)],
        compiler_params=_params("parallel", "arbitrary"),
        name="mla_attention",
    )(cq, ckv, kr, wq, wkv, cos, sin)


def _issue_row_gather(idx_of, src_hbm, n, sem_rows, dst_of, alternate=False):
    def issue(c, carry):
        for u in range(DMA_UNROLL):
            pltpu.make_async_copy(src_hbm.at[pl.ds(idx_of(c * DMA_UNROLL + u), 1)],
                                  dst_of(c, u), sem_rows).start(
                                      priority=u % 2 if alternate else 0)
        return carry
    lax.fori_loop(0, n // DMA_UNROLL, issue, 0)


def _for_row_counts(ns, tm, fn):
    for n in range(1, tm // MOE_SUB_M + 1):
        pl.when(ns == n)(functools.partial(fn, n * MOE_SUB_M))


def _moe_gu_kernel(te_ref, nu_ref, ns_ref, tok_hbm, x_hbm, wgu_ref, bgu_ref, cmp_ref,
                   o_ref, tok_smem, xbuf, xb, wgu_b, sem_idx, sem_rows):
    t, j = pl.program_id(0), pl.program_id(1)
    _, tm, half = xbuf.shape
    two_f = wgu_b.shape[1]
    sel = cmp_ref.shape[0]
    used = t < nu_ref[0]
    slot = t % 2

    def idx_copy(tile, sl):
        return pltpu.make_async_copy(tok_hbm.at[tile], tok_smem.at[sl], sem_idx.at[sl])

    def start_gather(tile, sl):
        _issue_row_gather(lambda a: tok_smem[sl, a], x_hbm, ns_ref[tile] * MOE_SUB_M,
                          sem_rows.at[sl],
                          lambda c, u: xbuf.at[sl, pl.ds(c * DMA_UNROLL + u, 1)])

    @pl.when(jnp.logical_and(used, j == 0))
    def _():
        @pl.when(t == 0)
        def _():
            idx_copy(0, 0).start()
            idx_copy(0, 0).wait()
            start_gather(0, 0)

            @pl.when(nu_ref[0] > 1)
            def _():
                idx_copy(1, 1).start()

        def land(rows):
            r = pl.ds(0, rows)
            pltpu.make_async_copy(x_hbm.at[r], xbuf.at[slot, r], sem_rows.at[slot]).wait()
            lo, hi = _unpack_bf16_halves(xbuf[slot, r, :])
            xb[r, :half] = lo
            xb[r, half:] = hi

        _for_row_counts(ns_ref[t], tm, land)

        @pl.when(t + 1 < nu_ref[0])
        def _():
            idx_copy(t + 1, 1 - slot).wait()
            start_gather(t + 1, 1 - slot)

            @pl.when(t + 2 < nu_ref[0])
            def _():
                idx_copy(t + 2, slot).start()

    @pl.when(jnp.logical_not(used))
    def _():
        o_ref[...] = jnp.zeros(o_ref.shape, o_ref.dtype)

    @pl.when(used)
    def _():
        def compute(rows):
            r = pl.ds(0, rows)
            x = xb[r, :]
            is_glu = (lax.broadcasted_iota(jnp.int32, (rows, sel), 1) & 1) == 0
            for c in range(0, two_f, sel):
                wgu_b[:, c:c + sel] = wgu_ref[0, 0, :, c:c + sel].astype(BF16)
                h = _dot(x, wgu_b[:, c:c + sel]) + bgu_ref[0, 0, :, c:c + sel]
                glu = jnp.minimum(h, SWIGLU_LIMIT)
                lin = jnp.clip(h, -SWIGLU_LIMIT, SWIGLU_LIMIT) + 1.0
                lin = jnp.concatenate(
                    [pltpu.roll(lin[:, v:v + LANES], LANES - 1, 1)
                     for v in range(0, sel, LANES)], axis=1)
                act = jnp.where(is_glu, glu * jax.nn.sigmoid(SWIGLU_ALPHA * glu) * lin, 0.0)
                o_ref[r, c // 2:(c + sel) // 2] = _dot(
                    act.astype(BF16), cmp_ref[...]).astype(o_ref.dtype)
            if rows < tm:
                o_ref[pl.ds(rows, tm - rows), :] = jnp.zeros(
                    (tm - rows, o_ref.shape[1]), o_ref.dtype)

        _for_row_counts(ns_ref[t], tm, compute)


def _moe_down_kernel(te_ref, nu_ref, ns_ref, a_ref, wd_ref, bd_ref, o_ref, wd_b):
    t = pl.program_id(1)
    tm = a_ref.shape[0]

    @pl.when(t >= nu_ref[0])
    def _():
        o_ref[...] = jnp.zeros(o_ref.shape, F32)

    @pl.when(t < nu_ref[0])
    def _():
        @pl.when(jnp.logical_or(t == 0, te_ref[t] != te_ref[jnp.maximum(t - 1, 0)]))
        def _():
            wd_b[...] = wd_ref[0, 0].astype(BF16)

        def compute(rows):
            r = pl.ds(0, rows)
            o_ref[r, :] = _dot(a_ref[r, :], wd_b[...]) + bd_ref[0, 0]
            if rows < tm:
                o_ref[pl.ds(rows, tm - rows), :] = jnp.zeros(
                    (tm - rows, o_ref.shape[1]), F32)

        _for_row_counts(ns_ref[t], tm, compute)


def _moe_experts(x_packed, tok_tiles, tile_expert, n_used, n_sub, w_gu, b_gu, w_down,
                 b_down, layer):
    n_tiles, tm = tok_tiles.shape
    half = x_packed.shape[1]
    d = 2 * half
    ff = w_down.shape[2]
    tf, tn = MOE_TILE_F, MOE_TILE_N
    n_f, n_n = ff // tf, d // tn
    sel = 4 * LANES
    compact = (jnp.arange(sel)[:, None] == 2 * jnp.arange(sel // 2)[None, :]).astype(BF16)

    def tile(i, nu):
        return jnp.minimum(i, nu[0] - 1)

    def chunk(i, j, nu, last):
        return jnp.where(i < nu[0], j, last)

    act = pl.pallas_call(
        _moe_gu_kernel,
        out_shape=jax.ShapeDtypeStruct((n_tiles * tm, ff), BF16),
        grid_spec=pltpu.PrefetchScalarGridSpec(
            num_scalar_prefetch=3,
            grid=(n_tiles, n_f),
            in_specs=[
                pl.BlockSpec(memory_space=pl.ANY),
                pl.BlockSpec(memory_space=pl.ANY),
                pl.BlockSpec((1, 1, d, 2 * tf), lambda i, j, te, nu, ns:
                             (layer, te[tile(i, nu)], 0, chunk(i, j, nu, n_f - 1))),
                pl.BlockSpec((1, 1, 1, 2 * tf), lambda i, j, te, nu, ns:
                             (layer, te[tile(i, nu)], 0, chunk(i, j, nu, n_f - 1))),
                pl.BlockSpec((sel, sel // 2), lambda i, j, te, nu, ns: (0, 0)),
            ],
            out_specs=pl.BlockSpec((tm, tf), lambda i, j, te, nu, ns: (i, j)),
            scratch_shapes=[pltpu.SMEM((2, tm), jnp.int32),
                            pltpu.VMEM((2, tm, half), jnp.uint32),
                            pltpu.VMEM((tm, d), BF16),
                            pltpu.VMEM((d, 2 * tf), BF16),
                            pltpu.SemaphoreType.DMA((2,)),
                            pltpu.SemaphoreType.DMA((2,))]),
        compiler_params=_params("arbitrary", "arbitrary"),
        name="moe_gate_up",
    )(tile_expert, n_used, n_sub, tok_tiles, x_packed, w_gu, b_gu[:, :, None, :], compact)

    return pl.pallas_call(
        _moe_down_kernel,
        out_shape=jax.ShapeDtypeStruct((n_tiles * tm, d), F32),
        grid_spec=pltpu.PrefetchScalarGridSpec(
            num_scalar_prefetch=3,
            grid=(n_n, n_tiles),
            in_specs=[
                pl.BlockSpec((tm, ff), lambda j, i, te, nu, ns: (tile(i, nu), 0)),
                pl.BlockSpec((1, 1, ff, tn), lambda j, i, te, nu, ns:
                             (layer, te[tile(i, nu)], 0, j)),
                pl.BlockSpec((1, 1, 1, tn), lambda j, i, te, nu, ns:
                             (layer, te[tile(i, nu)], 0, j)),
            ],
            out_specs=pl.BlockSpec((tm, tn), lambda j, i, te, nu, ns: (i, j)),
            scratch_shapes=[pltpu.VMEM((ff, tn), BF16)]),
        compiler_params=_params("arbitrary", "arbitrary"),
        name="moe_down",
    )(tile_expert, n_used, n_sub, act, w_down, b_down[:, :, None, :])


def _combine_kernel(pos_hbm, ys_hbm, gates_ref, x_ref, g_ref, b_ref, y_ref,
                    pos_smem, buf, sem_idx, sem_rows):
    i = pl.program_id(0)
    n = pl.num_programs(0)
    _, k, tc, _ = buf.shape
    per_slot = tc // DMA_UNROLL
    slot = i % 2

    def idx_copy(tile, sl):
        return pltpu.make_async_copy(pos_hbm.at[tile], pos_smem.at[sl], sem_idx.at[sl])

    def start_gather(sl):
        _issue_row_gather(
            lambda a: pos_smem[sl, a], ys_hbm, k * tc, sem_rows.at[sl],
            lambda c, u: buf.at[sl, c // per_slot,
                                pl.ds((c % per_slot) * DMA_UNROLL + u, 1)],
            alternate=True)

    @pl.when(i == 0)
    def _():
        idx_copy(0, 0).start()
        idx_copy(0, 0).wait()
        start_gather(0)

        @pl.when(n > 1)
        def _():
            idx_copy(1, 1).start()

    for kk in range(k):
        pltpu.make_async_copy(ys_hbm.at[pl.ds(0, tc)], buf.at[slot, kk],
                              sem_rows.at[slot]).wait()

    @pl.when(i + 1 < n)
    def _():
        idx_copy(i + 1, 1 - slot).wait()
        start_gather(1 - slot)

        @pl.when(i + 2 < n)
        def _():
            idx_copy(i + 2, slot).start()

    gates = gates_ref[...]
    h = sum(gates[:, kk:kk + 1] * buf[slot, kk] for kk in range(k))
    y_ref[...] = _layer_norm(ALPHA * x_ref[...] + h, g_ref[...], b_ref[...])


def _combine_ln(pos_tiles, ys, gates, x, g, b):
    t, d = x.shape
    tc = COMBINE_TILE
    row = lambda width: pl.BlockSpec((tc, width), lambda i: (i, 0))
    full = lambda a: pl.BlockSpec(a.shape, lambda i: (0,) * a.ndim)
    anyspec = pl.BlockSpec(memory_space=pl.ANY)
    return pl.pallas_call(
        _combine_kernel,
        out_shape=jax.ShapeDtypeStruct((t, d), F32),
        grid=(t // tc,),
        in_specs=[anyspec, anyspec, row(LANES), row(d), full(g), full(b)],
        out_specs=row(d),
        scratch_shapes=[pltpu.SMEM((2, MOE_TOP_K * tc), jnp.int32),
                        pltpu.VMEM((2, MOE_TOP_K, tc, d), F32),
                        pltpu.SemaphoreType.DMA((2,)),
                        pltpu.SemaphoreType.DMA((2,))],
        compiler_params=_params("arbitrary"),
        name="moe_combine_layernorm",
    )(pos_tiles, ys, gates, x, g, b)


def _rope_tables(seq, dim, slab):
    half = dim // 2
    pos = jnp.arange(seq, dtype=F32)
    inv = 1.0 / (ROPE_THETA ** (jnp.arange(0, dim, 2, dtype=F32) / dim))
    ang = pos[:, None] * inv[None, :]
    pad = jnp.zeros((seq, slab // 2 - half), F32)
    cos = jnp.concatenate([jnp.cos(ang), pad, jnp.cos(ang), pad], axis=1)
    sin = jnp.concatenate([-jnp.sin(ang), pad, jnp.sin(ang), pad], axis=1)
    return cos, sin


def _pad_rope_cols(w):
    half = MLA_ROPE_DIM // 2
    z = jnp.zeros(w.shape[:-1] + (LANES // 2 - half,), w.dtype)
    return jnp.concatenate([w[..., :half], z, w[..., half:], z], axis=-1)


def _route(top_e, tm, sub):
    t = top_e.shape[0]
    i32 = jnp.int32
    flat_e = top_e.reshape(-1)
    n_assign = t * MOE_TOP_K
    n_tiles = n_assign // tm + N_EXPERTS
    order = jnp.argsort(flat_e).astype(i32)
    sorted_e = flat_e[order]
    counts = jnp.sum(flat_e[:, None] == jnp.arange(N_EXPERTS, dtype=i32)[None, :],
                     axis=0, dtype=i32)
    padded = (counts + tm - 1) // tm * tm
    start = jnp.cumsum(counts) - counts
    pad_end = jnp.cumsum(padded)
    pad_start = pad_end - padded
    tile_start = jnp.arange(n_tiles, dtype=i32) * tm
    tile_expert = jnp.minimum(
        jnp.sum(pad_end[None, :] <= tile_start[:, None], axis=1, dtype=i32), N_EXPERTS - 1)
    n_used = (pad_end[-1:] // tm).astype(i32)
    first = tile_start - pad_start[tile_expert]
    tile_rows = jnp.clip(counts[tile_expert] - first, 0, tm)
    n_sub = (tile_rows + sub - 1) // sub
    k_in_e = first[:, None] + jnp.arange(tm, dtype=i32)[None, :]
    valid = k_in_e < counts[tile_expert][:, None]
    src = jnp.clip(start[tile_expert][:, None] + k_in_e, 0, n_assign - 1)
    row_token = jnp.where(valid, order[src] // MOE_TOP_K, 0).astype(i32)
    dest = (pad_start[sorted_e] + jnp.arange(n_assign, dtype=i32) - start[sorted_e]).astype(i32)
    _, pos = lax.sort_key_val(order, dest)
    return row_token, pos, tile_expert, n_used, n_sub.astype(i32)


def _moe_layer(x_packed, top_e, w_gu, b_gu, w_down, b_down, layer):
    tok_tiles, pos, tile_expert, n_used, n_sub = _route(top_e, MOE_TILE_M, MOE_SUB_M)
    ys = _moe_experts(x_packed, tok_tiles, tile_expert, n_used, n_sub,
                      w_gu, b_gu, w_down, b_down, layer)
    pos_tiles = pos.reshape(-1, COMBINE_TILE, MOE_TOP_K).transpose(0, 2, 1)
    return ys, pos_tiles.reshape(-1, MOE_TOP_K * COMBINE_TILE)


def _router_params(router_w, router_b):
    rw = jnp.pad(router_w, ((0, 0), (0, LANES - N_EXPERTS)))
    rw_hi = lax.reduce_precision(rw, exponent_bits=8, mantissa_bits=7)
    rw_lo = rw - rw_hi
    rb = jnp.pad(router_b, (0, LANES - N_EXPERTS), constant_values=MASK_VALUE)
    return jnp.concatenate([rw_hi, rw_lo], axis=1).astype(BF16), rb[None, :]


def kernel(x, ln_mix_g, ln_mix_b, ln_ffn_g, ln_ffn_b, w_in_ab, w_out_ab,
           mla_w_dq, mla_q_norm, mla_w_uq, mla_w_dkv, mla_kv_norm, mla_w_ukv, mla_w_out,
           router_w, router_b, moe_w_gu, moe_b_gu, moe_w_down, moe_b_down):
    b, s, d = x.shape
    t = b * s
    xt = x.reshape(t, d)
    cos_h, sin_h = _rope_tables(s, HEAD_DIM, LANES)
    cos_r, sin_r = _rope_tables(s, MLA_ROPE_DIM, LANES)
    row2 = lambda v: v[None, :]

    for layer in range(DEPTH):
        i = layer // 2
        rw, rb = _router_params(router_w[layer], router_b[layer])
        if layer % 2 == 0:
            proj = _matmul(xt, w_in_ab[i].astype(BF16)).reshape(b, s, -1)
            o_a = _longnet(proj, cos_h, sin_h).reshape(t, -1)
            o_b = _moba(proj, cos_h, sin_h).reshape(t, -1)
            w_out = w_out_ab[i].astype(BF16)
            split = A_HEADS * HEAD_DIM
            outs, ws = [o_a, o_b], [w_out[:split], w_out[split:]]
        else:
            w_cat = jnp.concatenate(
                [mla_w_dq[i], mla_w_dkv[i][:, :MLA_KV_RANK],
                 _pad_rope_cols(mla_w_dkv[i][:, MLA_KV_RANK:])], axis=1).astype(BF16)
            cq, ckv, kr = _mla_down(xt, w_cat, row2(mla_q_norm[i]), row2(mla_kv_norm[i]))
            wq = mla_w_uq[i].reshape(MLA_Q_RANK, MLA_HEADS, MLA_NOPE_DIM + MLA_ROPE_DIM)
            wq = jnp.concatenate([wq[..., :MLA_NOPE_DIM],
                                  _pad_rope_cols(wq[..., MLA_NOPE_DIM:])], axis=-1)
            wq = wq.reshape(MLA_Q_RANK, MLA_HEADS * 2 * LANES).astype(BF16)
            o = _mla_attn(cq.reshape(b, s, -1), ckv.reshape(b, s, -1), kr.reshape(b, s, -1),
                          wq, mla_w_ukv[i].astype(BF16), cos_r, sin_r)
            outs, ws = [o.reshape(t, -1)], [mla_w_out[i].astype(BF16)]
        xt, x_packed, top_e, gates = _outproj_ln(outs, ws, xt, row2(ln_mix_g[layer]),
                                                 row2(ln_mix_b[layer]), rw, rb)
        ys, pos_tiles = _moe_layer(x_packed, top_e[:, :MOE_TOP_K], moe_w_gu, moe_b_gu,
                                   moe_w_down, moe_b_down, layer)
        xt = _combine_ln(pos_tiles, ys, gates, xt,
                         row2(ln_ffn_g[layer]), row2(ln_ffn_b[layer]))
    return xt.reshape(b, s, d)
```

```python
import functools

import jax
import jax.numpy as jnp
from jax import lax
from jax.experimental import pallas as pl
from jax.experimental.pallas import tpu as pltpu

F32 = jnp.float32
BF16 = jnp.bfloat16

DEPTH = 2
HEAD_DIM = 128
A_HEADS = 12
B_HEADS = 4
DILATED_CONFIGS = ((128, 1), (512, 4), (2048, 16))
MOBA_BLOCK = 256
MOBA_TOP_K = 3
ROPE_THETA = 10000.0
MLA_HEADS = 16
MLA_Q_RANK = 512
MLA_KV_RANK = 512
MLA_NOPE_DIM = 128
MLA_ROPE_DIM = 64
MLA_V_DIM = 128
N_EXPERTS = 32
MOE_TOP_K = 4
SWIGLU_LIMIT = 7.0
SWIGLU_ALPHA = 1.702
LN_EPS = 1e-5
RMS_EPS = 1e-6
ALPHA = (2.0 * DEPTH) ** 0.25

LANES = 128
MASK_VALUE = -1e30
VMEM_LIMIT = 56 * 1024 * 1024

MM_TILE_M = 1024
MM_TILE_N = 1024
LN_TILE_M = 256
ATT_TILE_Q = 256
LONGNET_GROUP = 8
MOE_TILE_M = 1024
MOE_SUB_M = 256
MOE_TILE_F = 512
MOE_TILE_N = 1024
COMBINE_TILE = 256
DMA_UNROLL = 16


def _params(*semantics):
    return pltpu.CompilerParams(dimension_semantics=semantics,
                                vmem_limit_bytes=VMEM_LIMIT)


def _dot(a, b):
    return jnp.dot(a, b, preferred_element_type=F32)


def _dot_nt(a, b):
    return lax.dot_general(a, b, (((1,), (1,)), ((), ())),
                           preferred_element_type=F32)


def _dot_nt_f32(a, b):
    return lax.dot_general(a, b, (((1,), (1,)), ((), ())),
                           precision=lax.Precision.HIGHEST,
                           preferred_element_type=F32)


def _exp_bf16(z):
    return jnp.exp(z.astype(BF16))


def _with_ones(v):
    return jnp.concatenate([v, jnp.ones(v.shape, v.dtype)], axis=-1)


def _rope(x, cos, sin):
    return x * cos + pltpu.roll(x, LANES // 2, 1) * sin


def _mm_kernel(x_ref, w_ref, o_ref, xb_ref):
    @pl.when(pl.program_id(1) == 0)
    def _():
        xb_ref[...] = x_ref[...].astype(BF16)
    o_ref[...] = _dot(xb_ref[...], w_ref[...])


def _matmul(x, w):
    m, k = x.shape
    n = w.shape[1]
    tm, tn = min(MM_TILE_M, m), min(MM_TILE_N, n)
    return pl.pallas_call(
        _mm_kernel,
        out_shape=jax.ShapeDtypeStruct((m, n), F32),
        grid=(m // tm, n // tn),
        in_specs=[pl.BlockSpec((tm, k), lambda i, j: (i, 0)),
                  pl.BlockSpec((k, tn), lambda i, j: (0, j))],
        out_specs=pl.BlockSpec((tm, tn), lambda i, j: (i, j)),
        scratch_shapes=[pltpu.VMEM((tm, k), BF16)],
        compiler_params=_params("parallel", "arbitrary"),
        name="proj_matmul",
    )(x, w)


def _longnet_kernel(q_ref, k_ref, v_ref, cos_ref, sin_ref, o_ref,
                    qs, ks, vs, oc, lc, *, seq):
    band = DILATED_CONFIGS[0][0] // DILATED_CONFIGS[0][1]
    cos, sin = cos_ref[...], sin_ref[...]
    qs[...] = _rope(q_ref[0], cos, sin) * (HEAD_DIM ** -0.5)
    ks[...] = _rope(k_ref[0], cos, sin)
    vs[...] = v_ref[0]

    def band_mask(g, with_prev):
        width = 2 * band if with_prev else band
        qi = lax.broadcasted_iota(jnp.int32, (g, band, width), 1)
        kj = lax.broadcasted_iota(jnp.int32, (g, band, width), 2)
        if with_prev:
            return jnp.logical_and(kj >= qi, kj <= qi + band)
        return kj <= qi

    def rows(dil, start, size):
        return pl.ds(start, size) if dil == 1 else pl.ds(start, size, stride=dil)

    first_blocks, later_blocks = [], []
    for c, (window, dil) in enumerate(DILATED_CONFIGS):
        assert window // dil == band and seq % (dil * band) == 0
        for r in range(dil):
            for n in range(seq // (dil * band)):
                (later_blocks if n else first_blocks).append((c, dil, r + dil * band * n))

    def attend(group, with_prev):
        def stacked(ref, back, size):
            return jnp.stack([ref[rows(d, st - back * d * band, size), :]
                              for _, d, st in group]).astype(BF16)
        qb = stacked(qs, 0, band)
        kb = stacked(ks, 1, 2 * band) if with_prev else stacked(ks, 0, band)
        vb = stacked(vs, 1, 2 * band) if with_prev else stacked(vs, 0, band)
        s = lax.dot_general(qb, kb, (((2,), (2,)), ((0,), (0,))),
                            preferred_element_type=F32)
        s = jnp.where(band_mask(len(group), with_prev), s, MASK_VALUE)
        m = jnp.max(s, axis=-1, keepdims=True)
        acc = lax.dot_general(_exp_bf16(s - m), _with_ones(vb), (((2,), (1,)), ((0,), (0,))),
                              preferred_element_type=F32)
        l = acc[:, :, HEAD_DIM:]
        o = acc[:, :, :HEAD_DIM] / l
        lse = m + jnp.log(l[:, :, :1])
        for g, (c, d, st) in enumerate(group):
            oc[c, rows(d, st, band), :] = o[g]
            lc[c, rows(d, st, band), :] = jnp.broadcast_to(lse[g], (band, LANES))

    for blocks, with_prev in ((first_blocks, False), (later_blocks, True)):
        for i in range(0, len(blocks), LONGNET_GROUP):
            attend(blocks[i:i + LONGNET_GROUP], with_prev)

    lses = [lc[c] for c in range(len(DILATED_CONFIGS))]
    top = functools.reduce(jnp.maximum, lses)
    es = [jnp.exp(v - top) for v in lses]
    num = sum(e * oc[c] for c, e in enumerate(es))
    o_ref[0] = num / sum(es)


def _longnet(proj, cos, sin):
    b, s, _ = proj.shape
    n_cfg = len(DILATED_CONFIGS)
    blk = lambda off: pl.BlockSpec((1, s, HEAD_DIM), lambda i, h: (i, 0, off + h))
    tab = pl.BlockSpec((s, HEAD_DIM), lambda i, h: (0, 0))
    return pl.pallas_call(
        functools.partial(_longnet_kernel, seq=s),
        out_shape=jax.ShapeDtypeStruct((b, s, A_HEADS * HEAD_DIM), F32),
        grid=(b, A_HEADS),
        in_specs=[blk(0), blk(A_HEADS), blk(2 * A_HEADS), tab, tab],
        out_specs=pl.BlockSpec((1, s, HEAD_DIM), lambda i, h: (i, 0, h)),
        scratch_shapes=[pltpu.VMEM((s, HEAD_DIM), F32),
                        pltpu.VMEM((s, HEAD_DIM), F32),
                        pltpu.VMEM((s, HEAD_DIM), F32),
                        pltpu.VMEM((n_cfg, s, HEAD_DIM), F32),
                        pltpu.VMEM((n_cfg, s, LANES), F32)],
        compiler_params=_params("parallel", "parallel"),
        name="longnet_attention",
    )(proj, proj, proj, cos, sin)


def _softmax_blocks(qb, k_ref, v1_ref, blocks, tile):
    scores = []
    m = None
    for start, mask in blocks:
        s = _dot_nt(qb, k_ref[pl.ds(start, tile), :])
        if mask is not None:
            s = jnp.where(mask, s, MASK_VALUE)
        scores.append(s)
        bm = jnp.max(s, axis=-1, keepdims=True)
        m = bm if m is None else jnp.maximum(m, bm)
    acc = None
    for (start, _), s in zip(blocks, scores):
        pv = _dot(_exp_bf16(s - m), v1_ref[pl.ds(start, tile), :])
        acc = pv if acc is None else acc + pv
    half = acc.shape[1] // 2
    return acc[:, :half] / acc[:, half:]


def _moba_kernel(q_ref, k_ref, v_ref, cos_ref, sin_ref, o_ref, qs, ks, vs, *, seq):
    blk = MOBA_BLOCK
    nblk = seq // blk
    cos, sin = cos_ref[...], sin_ref[...]
    q = _rope(q_ref[0], cos, sin)
    k = _rope(k_ref[0], cos, sin)
    qs[...] = (q * (HEAD_DIM ** -0.5)).astype(BF16)
    ks[...] = k.astype(BF16)
    vs[...] = _with_ones(v_ref[0].astype(BF16))
    k_mean = jnp.concatenate(
        [jnp.mean(k[n * blk:(n + 1) * blk], axis=0, keepdims=True) for n in range(nblk)],
        axis=0)
    gate = _dot_nt_f32(q, k_mean)
    row = lax.broadcasted_iota(jnp.int32, (blk, blk), 0)
    col = lax.broadcasted_iota(jnp.int32, (blk, blk), 1)
    causal = col <= row
    n_top = min(MOBA_TOP_K, nblk)

    for i in range(nblk):
        g = gate[i * blk:(i + 1) * blk, :]
        blocks = [(i * blk, causal)]
        for n in range(i):
            if i <= n_top:
                blocks.append((n * blk, None))
                continue
            gn = g[:, n:n + 1]
            rank = jnp.zeros((blk, 1), jnp.int32)
            for o in range(i):
                if o == n:
                    continue
                go = g[:, o:o + 1]
                ahead = (go > gn) | (go == gn) if o < n else (go > gn)
                rank = rank + ahead.astype(jnp.int32)
            blocks.append((n * blk, rank < n_top))
        o_ref[0, pl.ds(i * blk, blk), :] = _softmax_blocks(
            qs[pl.ds(i * blk, blk), :], ks, vs, blocks, blk)


def _moba(proj, cos, sin):
    b, s, _ = proj.shape
    base = 3 * A_HEADS
    blk = lambda off: pl.BlockSpec((1, s, HEAD_DIM), lambda i, h: (i, 0, base + off + h))
    tab = pl.BlockSpec((s, HEAD_DIM), lambda i, h: (0, 0))
    return pl.pallas_call(
        functools.partial(_moba_kernel, seq=s),
        out_shape=jax.ShapeDtypeStruct((b, s, B_HEADS * HEAD_DIM), F32),
        grid=(b, B_HEADS),
        in_specs=[blk(0), blk(B_HEADS), blk(2 * B_HEADS), tab, tab],
        out_specs=pl.BlockSpec((1, s, HEAD_DIM), lambda i, h: (i, 0, h)),
        scratch_shapes=[pltpu.VMEM((s, HEAD_DIM), BF16)] * 2
        + [pltpu.VMEM((s, 2 * HEAD_DIM), BF16)],
        compiler_params=_params("parallel", "parallel"),
        name="moba_attention",
    )(proj, proj, proj, cos, sin)


def _layer_norm(z, g, b):
    mu = jnp.mean(z, axis=-1, keepdims=True)
    zc = z - mu
    var = jnp.mean(zc * zc, axis=-1, keepdims=True)
    return zc * lax.rsqrt(var + LN_EPS) * g + b


def _route_topk(y, rw_ref, rb_ref, e_ref, g_ref):
    y_hi = y.astype(BF16)
    y_lo = (y - y_hi.astype(F32)).astype(BF16)
    hi_both = _dot(y_hi, rw_ref[...])
    lo_hi = _dot(y_lo, rw_ref[:, :LANES])
    logits = hi_both[:, :LANES] + (hi_both[:, LANES:] + lo_hi) + rb_ref[...]
    lane = lax.broadcasted_iota(jnp.int32, logits.shape, 1)
    vals, idxs = [], []
    for _ in range(MOE_TOP_K):
        v = jnp.max(logits, axis=-1, keepdims=True)
        ix = jnp.min(jnp.where(logits == v, lane, LANES), axis=-1, keepdims=True)
        vals.append(v)
        idxs.append(ix)
        logits = jnp.where(lane == ix, -jnp.inf, logits)
    exps = [jnp.exp(v - vals[0]) for v in vals]
    denom = sum(exps)
    e_out = jnp.zeros(lane.shape, jnp.int32)
    g_out = jnp.zeros(lane.shape, F32)
    for kk in range(MOE_TOP_K):
        e_out = jnp.where(lane == kk, idxs[kk], e_out)
        g_out = jnp.where(lane == kk, exps[kk] / denom, g_out)
    e_ref[...] = e_out
    g_ref[...] = g_out


_HI16 = 0xFFFF0000


def _pack_bf16_halves(y):
    half = y.shape[1] // 2
    bits = lambda v: lax.bitcast_convert_type(v.astype(BF16).astype(F32), jnp.uint32)
    return (bits(y[:, :half]) >> 16) | (bits(y[:, half:]) & jnp.uint32(_HI16))


def _unpack_bf16_halves(pk):
    lo = lax.bitcast_convert_type(pk << 16, F32).astype(BF16)
    hi = lax.bitcast_convert_type(pk & jnp.uint32(_HI16), F32).astype(BF16)
    return lo, hi


def _outproj_ln_kernel(*refs, n_in):
    o_refs, w_refs = refs[:n_in], refs[n_in:2 * n_in]
    x_ref, g_ref, b_ref, rw_ref, rb_ref, y_ref, pk_ref, e_ref, gt_ref = refs[2 * n_in:]
    h = sum(_dot(o[...].astype(BF16), w[...]) for o, w in zip(o_refs, w_refs))
    y = _layer_norm(ALPHA * x_ref[...] + h, g_ref[...], b_ref[...])
    y_ref[...] = y
    pk_ref[...] = _pack_bf16_halves(y)
    _route_topk(y, rw_ref, rb_ref, e_ref, gt_ref)


def _outproj_ln(outs, ws, x, g, b, rw, rb):
    t, d = x.shape
    tm = LN_TILE_M
    n_in = len(outs)
    row = lambda width: pl.BlockSpec((tm, width), lambda i: (i, 0))
    full = lambda a: pl.BlockSpec(a.shape, lambda i: (0,) * a.ndim)
    return pl.pallas_call(
        functools.partial(_outproj_ln_kernel, n_in=n_in),
        out_shape=(jax.ShapeDtypeStruct((t, d), F32),
                   jax.ShapeDtypeStruct((t, d // 2), jnp.uint32),
                   jax.ShapeDtypeStruct((t, LANES), jnp.int32),
                   jax.ShapeDtypeStruct((t, LANES), F32)),
        grid=(t // tm,),
        in_specs=([row(o.shape[1]) for o in outs] + [full(w) for w in ws]
                  + [row(d), full(g), full(b), full(rw), full(rb)]),
        out_specs=(row(d), row(d // 2), row(LANES), row(LANES)),
        compiler_params=_params("parallel"),
        name="outproj_layernorm_router",
    )(*outs, *ws, x, g, b, rw, rb)


def _rms_norm(x, g):
    return x * lax.rsqrt(jnp.mean(x * x, axis=-1, keepdims=True) + RMS_EPS) * g


def _mla_down_kernel(x_ref, w_ref, qn_ref, kvn_ref, cq_ref, ckv_ref, kr_ref):
    a = _dot(x_ref[...].astype(BF16), w_ref[...])
    cq_ref[...] = _rms_norm(a[:, :MLA_Q_RANK], qn_ref[...]).astype(BF16)
    ckv_ref[...] = _rms_norm(a[:, MLA_Q_RANK:MLA_Q_RANK + MLA_KV_RANK],
                             kvn_ref[...]).astype(BF16)
    kr_ref[...] = a[:, MLA_Q_RANK + MLA_KV_RANK:]


def _mla_down(x, w_cat, q_norm, kv_norm):
    t, d = x.shape
    tm = MM_TILE_M
    row = lambda width: pl.BlockSpec((tm, width), lambda i: (i, 0))
    full = lambda a: pl.BlockSpec(a.shape, lambda i: (0,) * a.ndim)
    return pl.pallas_call(
        _mla_down_kernel,
        out_shape=(jax.ShapeDtypeStruct((t, MLA_Q_RANK), BF16),
                   jax.ShapeDtypeStruct((t, MLA_KV_RANK), BF16),
                   jax.ShapeDtypeStruct((t, LANES), F32)),
        grid=(t // tm,),
        in_specs=[row(d), full(w_cat), full(q_norm), full(kv_norm)],
        out_specs=(row(MLA_Q_RANK), row(MLA_KV_RANK), row(LANES)),
        compiler_params=_params("parallel"),
        name="mla_down_rmsnorm",
    )(x, w_cat, q_norm, kv_norm)


def _mla_attn_kernel(cq_ref, ckv_ref, kr_ref, wq_ref, wkv_ref, cos_ref, sin_ref,
                     o_ref, qs, ks, vs, *, seq):
    tile = ATT_TILE_Q
    scale = (MLA_NOPE_DIM + MLA_ROPE_DIM) ** -0.5
    cos, sin = cos_ref[...], sin_ref[...]
    q = _dot(cq_ref[0], wq_ref[...])
    kv = _dot(ckv_ref[0], wkv_ref[...])
    qs[:, :LANES] = (q[:, :LANES] * scale).astype(BF16)
    qs[:, LANES:] = (_rope(q[:, LANES:], cos, sin) * scale).astype(BF16)
    ks[:, :LANES] = kv[:, :MLA_NOPE_DIM].astype(BF16)
    ks[:, LANES:] = _rope(kr_ref[0], cos, sin).astype(BF16)
    vs[...] = _with_ones(kv[:, MLA_NOPE_DIM:].astype(BF16))
    row = lax.broadcasted_iota(jnp.int32, (tile, tile), 0)
    col = lax.broadcasted_iota(jnp.int32, (tile, tile), 1)
    causal = col <= row
    for i in range(seq // tile):
        past = i * tile
        keys = pl.ds(0, past + tile)
        s = _dot_nt(qs[pl.ds(past, tile), :], ks[keys, :])
        s_diag = jnp.where(causal, s[:, past:], MASK_VALUE)
        m = jnp.max(s_diag, axis=-1, keepdims=True)
        if i:
            m = jnp.maximum(m, jnp.max(s[:, :past], axis=-1, keepdims=True))
            p = jnp.concatenate([_exp_bf16(s[:, :past] - m), _exp_bf16(s_diag - m)], axis=1)
        else:
            p = _exp_bf16(s_diag - m)
        acc = _dot(p, vs[keys, :])
        o_ref[0, pl.ds(past, tile), :] = acc[:, :MLA_V_DIM] / acc[:, MLA_V_DIM:]


def _mla_attn(cq, ckv, kr, wq, wkv, cos, sin):
    b, s, _ = cq.shape
    per_b = lambda width: pl.BlockSpec((1, s, width), lambda i, h: (i, 0, 0))
    head_w = lambda a: pl.BlockSpec((a.shape[0], 2 * LANES), lambda i, h: (0, h))
    tab = pl.BlockSpec((s, LANES), lambda i, h: (0, 0))
    return pl.pallas_call(
        functools.partial(_mla_attn_kernel, seq=s),
        out_shape=jax.ShapeDtypeStruct((b, s, MLA_HEADS * MLA_V_DIM), F32),
        grid=(b, MLA_HEADS),
        in_specs=[per_b(MLA_Q_RANK), per_b(MLA_KV_RANK), per_b(LANES),
                  head_w(wq), head_w(wkv), tab, tab],
        out_specs=pl.BlockSpec((1, s, MLA_V_DIM), lambda i, h: (i, 0, h)),
        scratch_shapes=[pltpu.VMEM((s, 2 * LANES), BF16),
                        pltpu.VMEM((s, 2 * LANES), BF16),
                        pltpu.VMEM((s, 2 * MLA_V_DIM), BF16)],
        compiler_params=_params("parallel", "arbitrary"),
        name="mla_attention",
    )(cq, ckv, kr, wq, wkv, cos, sin)


def _issue_row_gather(idx_of, src_hbm, n, sem_rows, dst_of, alternate=False):
    def issue(c, carry):
        base = pl.multiple_of(c * DMA_UNROLL, DMA_UNROLL)
        for u in range(DMA_UNROLL):
            pltpu.make_async_copy(src_hbm.at[pl.ds(idx_of(base + u), 1)],
                                  dst_of(base, u), sem_rows).start(
                                      priority=u % 2 if alternate else 0)
        return carry
    lax.fori_loop(0, n // DMA_UNROLL, issue, 0)


def _for_row_counts(ns, tm, fn):
    for n in range(1, tm // MOE_SUB_M + 1):
        pl.when(ns == n)(functools.partial(fn, n * MOE_SUB_M))


def _moe_gu_kernel(te_ref, nu_ref, ns_ref, tok_hbm, x_hbm, wgu_ref, bgu_ref, cmp_ref,
                   o_ref, tok_smem, xbuf, xb, wgu_b, sem_idx, sem_rows):
    t, j = pl.program_id(0), pl.program_id(1)
    _, tm, half = xbuf.shape
    two_f = wgu_b.shape[1]
    sel = cmp_ref.shape[0]
    used = t < nu_ref[0]
    slot = t % 2

    def idx_copy(tile, sl):
        return pltpu.make_async_copy(
            tok_hbm.at[tile], tok_smem.at[pl.ds(pl.multiple_of(sl * tm, tm), tm)],
            sem_idx.at[sl])

    def start_gather(tile, sl):
        _issue_row_gather(lambda a: tok_smem[sl * tm + a], x_hbm,
                          ns_ref[tile] * MOE_SUB_M, sem_rows.at[sl],
                          lambda base, u: xbuf.at[sl, pl.ds(base + u, 1)])

    @pl.when(jnp.logical_and(used, j == 0))
    def _():
        @pl.when(t == 0)
        def _():
            idx_copy(0, 0).start()
            idx_copy(0, 0).wait()
            start_gather(0, 0)

            @pl.when(nu_ref[0] > 1)
            def _():
                idx_copy(1, 1).start()

        def land(rows):
            r = pl.ds(0, rows)
            pltpu.make_async_copy(x_hbm.at[r], xbuf.at[slot, r], sem_rows.at[slot]).wait()
            lo, hi = _unpack_bf16_halves(xbuf[slot, r, :])
            xb[r, :half] = lo
            xb[r, half:] = hi

        _for_row_counts(ns_ref[t], tm, land)

        @pl.when(t + 1 < nu_ref[0])
        def _():
            idx_copy(t + 1, 1 - slot).wait()
            start_gather(t + 1, 1 - slot)

            @pl.when(t + 2 < nu_ref[0])
            def _():
                idx_copy(t + 2, slot).start()

    @pl.when(jnp.logical_not(used))
    def _():
        o_ref[...] = jnp.zeros(o_ref.shape, o_ref.dtype)

    @pl.when(used)
    def _():
        def compute(rows):
            r = pl.ds(0, rows)
            x = xb[r, :]
            is_glu = (lax.broadcasted_iota(jnp.int32, (rows, sel), 1) & 1) == 0
            for c in range(0, two_f, sel):
                wgu_b[:, c:c + sel] = wgu_ref[0, 0, :, c:c + sel].astype(BF16)
                h = _dot(x, wgu_b[:, c:c + sel]) + bgu_ref[0, 0, :, c:c + sel]
                glu = jnp.minimum(h, SWIGLU_LIMIT)
                lin = jnp.clip(h, -SWIGLU_LIMIT, SWIGLU_LIMIT) + 1.0
                lin = jnp.concatenate(
                    [pltpu.roll(lin[:, v:v + LANES], LANES - 1, 1)
                     for v in range(0, sel, LANES)], axis=1)
                act = jnp.where(is_glu, glu * jax.nn.sigmoid(SWIGLU_ALPHA * glu) * lin, 0.0)
                o_ref[r, c // 2:(c + sel) // 2] = _dot(
                    act.astype(BF16), cmp_ref[...]).astype(o_ref.dtype)
            if rows < tm:
                o_ref[pl.ds(rows, tm - rows), :] = jnp.zeros(
                    (tm - rows, o_ref.shape[1]), o_ref.dtype)

        _for_row_counts(ns_ref[t], tm, compute)


def _moe_down_kernel(te_ref, nu_ref, ns_ref, a_ref, wd_ref, bd_ref, o_ref, wd_b):
    t = pl.program_id(1)
    tm = a_ref.shape[0]

    @pl.when(t >= nu_ref[0])
    def _():
        o_ref[...] = jnp.zeros(o_ref.shape, F32)

    @pl.when(t < nu_ref[0])
    def _():
        @pl.when(jnp.logical_or(t == 0, te_ref[t] != te_ref[jnp.maximum(t - 1, 0)]))
        def _():
            wd_b[...] = wd_ref[0, 0].astype(BF16)

        def compute(rows):
            r = pl.ds(0, rows)
            o_ref[r, :] = _dot(a_ref[r, :], wd_b[...]) + bd_ref[0, 0]
            if rows < tm:
                o_ref[pl.ds(rows, tm - rows), :] = jnp.zeros(
                    (tm - rows, o_ref.shape[1]), F32)

        _for_row_counts(ns_ref[t], tm, compute)


def _moe_experts(x_packed, tok_tiles, tile_expert, n_used, n_sub, w_gu, b_gu, w_down,
                 b_down, layer):
    n_tiles, tm = tok_tiles.shape
    half = x_packed.shape[1]
    d = 2 * half
    ff = w_down.shape[2]
    tf, tn = MOE_TILE_F, MOE_TILE_N
    n_f, n_n = ff // tf, d // tn
    sel = 4 * LANES
    compact = (jnp.arange(sel)[:, None] == 2 * jnp.arange(sel // 2)[None, :]).astype(BF16)

    def tile(i, nu):
        return jnp.minimum(i, nu[0] - 1)

    def chunk(i, j, nu, last):
        return jnp.where(i < nu[0], j, last)

    act = pl.pallas_call(
        _moe_gu_kernel,
        out_shape=jax.ShapeDtypeStruct((n_tiles * tm, ff), BF16),
        grid_spec=pltpu.PrefetchScalarGridSpec(
            num_scalar_prefetch=3,
            grid=(n_tiles, n_f),
            in_specs=[
                pl.BlockSpec(memory_space=pl.ANY),
                pl.BlockSpec(memory_space=pl.ANY),
                pl.BlockSpec((1, 1, d, 2 * tf), lambda i, j, te, nu, ns:
                             (layer, te[tile(i, nu)], 0, chunk(i, j, nu, n_f - 1))),
                pl.BlockSpec((1, 1, 1, 2 * tf), lambda i, j, te, nu, ns:
                             (layer, te[tile(i, nu)], 0, chunk(i, j, nu, n_f - 1))),
                pl.BlockSpec((sel, sel // 2), lambda i, j, te, nu, ns: (0, 0)),
            ],
            out_specs=pl.BlockSpec((tm, tf), lambda i, j, te, nu, ns: (i, j)),
            scratch_shapes=[pltpu.SMEM((2 * tm,), jnp.int32),
                            pltpu.VMEM((2, tm, half), jnp.uint32),
                            pltpu.VMEM((tm, d), BF16),
                            pltpu.VMEM((d, 2 * tf), BF16),
                            pltpu.SemaphoreType.DMA((2,)),
                            pltpu.SemaphoreType.DMA((2,))]),
        compiler_params=_params("arbitrary", "arbitrary"),
        name="moe_gate_up",
    )(tile_expert, n_used, n_sub, tok_tiles, x_packed, w_gu, b_gu[:, :, None, :], compact)

    return pl.pallas_call(
        _moe_down_kernel,
        out_shape=jax.ShapeDtypeStruct((n_tiles * tm, d), F32),
        grid_spec=pltpu.PrefetchScalarGridSpec(
            num_scalar_prefetch=3,
            grid=(n_n, n_tiles),
            in_specs=[
                pl.BlockSpec((tm, ff), lambda j, i, te, nu, ns: (tile(i, nu), 0)),
                pl.BlockSpec((1, 1, ff, tn), lambda j, i, te, nu, ns:
                             (layer, te[tile(i, nu)], 0, j)),
                pl.BlockSpec((1, 1, 1, tn), lambda j, i, te, nu, ns:
                             (layer, te[tile(i, nu)], 0, j)),
            ],
            out_specs=pl.BlockSpec((tm, tn), lambda j, i, te, nu, ns: (i, j)),
            scratch_shapes=[pltpu.VMEM((ff, tn), BF16)]),
        compiler_params=_params("arbitrary", "arbitrary"),
        name="moe_down",
    )(tile_expert, n_used, n_sub, act, w_down, b_down[:, :, None, :])


def _combine_kernel(pos_hbm, ys_hbm, gates_ref, x_ref, g_ref, b_ref, y_ref,
                    pos_smem, buf, sem_idx, sem_rows):
    i = pl.program_id(0)
    n = pl.num_programs(0)
    _, k, tc, _ = buf.shape
    slot = i % 2

    def idx_copy(tile, sl):
        return pltpu.make_async_copy(
            pos_hbm.at[tile], pos_smem.at[pl.ds(pl.multiple_of(sl * k * tc, k * tc), k * tc)],
            sem_idx.at[sl])

    def start_gather(sl):
        _issue_row_gather(
            lambda a: pos_smem[sl * k * tc + a], ys_hbm, k * tc, sem_rows.at[sl],
            lambda base, u: buf.at[sl, base // tc,
                                   pl.ds(pl.multiple_of(base % tc, DMA_UNROLL) + u, 1)],
            alternate=True)

    @pl.when(i == 0)
    def _():
        idx_copy(0, 0).start()
        idx_copy(0, 0).wait()
        start_gather(0)

        @pl.when(n > 1)
        def _():
            idx_copy(1, 1).start()

    for kk in range(k):
        pltpu.make_async_copy(ys_hbm.at[pl.ds(0, tc)], buf.at[slot, kk],
                              sem_rows.at[slot]).wait()

    @pl.when(i + 1 < n)
    def _():
        idx_copy(i + 1, 1 - slot).wait()
        start_gather(1 - slot)

        @pl.when(i + 2 < n)
        def _():
            idx_copy(i + 2, slot).start()

    gates = gates_ref[...]
    h = sum(gates[:, kk:kk + 1] * buf[slot, kk] for kk in range(k))
    y_ref[...] = _layer_norm(ALPHA * x_ref[...] + h, g_ref[...], b_ref[...])


def _combine_ln(pos_tiles, ys, gates, x, g, b):
    t, d = x.shape
    tc = COMBINE_TILE
    row = lambda width: pl.BlockSpec((tc, width), lambda i: (i, 0))
    full = lambda a: pl.BlockSpec(a.shape, lambda i: (0,) * a.ndim)
    anyspec = pl.BlockSpec(memory_space=pl.ANY)
    return pl.pallas_call(
        _combine_kernel,
        out_shape=jax.ShapeDtypeStruct((t, d), F32),
        grid=(t // tc,),
        in_specs=[anyspec, anyspec, row(LANES), row(d), full(g), full(b)],
        out_specs=row(d),
        scratch_shapes=[pltpu.SMEM((2 * MOE_TOP_K * tc,), jnp.int32),
                        pltpu.VMEM((2, MOE_TOP_K, tc, d), F32),
                        pltpu.SemaphoreType.DMA((2,)),
                        pltpu.SemaphoreType.DMA((2,))],
        compiler_params=_params("arbitrary"),
        name="moe_combine_layernorm",
    )(pos_tiles, ys, gates, x, g, b)


def _rope_tables(seq, dim, slab):
    half = dim // 2
    pos = jnp.arange(seq, dtype=F32)
    inv = 1.0 / (ROPE_THETA ** (jnp.arange(0, dim, 2, dtype=F32) / dim))
    ang = pos[:, None] * inv[None, :]
    pad = jnp.zeros((seq, slab // 2 - half), F32)
    cos = jnp.concatenate([jnp.cos(ang), pad, jnp.cos(ang), pad], axis=1)
    sin = jnp.concatenate([-jnp.sin(ang), pad, jnp.sin(ang), pad], axis=1)
    return cos, sin


def _pad_rope_cols(w):
    half = MLA_ROPE_DIM // 2
    z = jnp.zeros(w.shape[:-1] + (LANES // 2 - half,), w.dtype)
    return jnp.concatenate([w[..., :half], z, w[..., half:], z], axis=-1)


def _route(top_e, tm, sub):
    t = top_e.shape[0]
    i32 = jnp.int32
    flat_e = top_e.reshape(-1)
    n_assign = t * MOE_TOP_K
    n_tiles = n_assign // tm + N_EXPERTS
    order = jnp.argsort(flat_e).astype(i32)
    sorted_e = flat_e[order]
    counts = jnp.sum(flat_e[:, None] == jnp.arange(N_EXPERTS, dtype=i32)[None, :],
                     axis=0, dtype=i32)
    padded = (counts + tm - 1) // tm * tm
    start = jnp.cumsum(counts) - counts
    pad_end = jnp.cumsum(padded)
    pad_start = pad_end - padded
    tile_start = jnp.arange(n_tiles, dtype=i32) * tm
    tile_expert = jnp.minimum(
        jnp.sum(pad_end[None, :] <= tile_start[:, None], axis=1, dtype=i32), N_EXPERTS - 1)
    n_used = (pad_end[-1:] // tm).astype(i32)
    first = tile_start - pad_start[tile_expert]
    tile_rows = jnp.clip(counts[tile_expert] - first, 0, tm)
    n_sub = (tile_rows + sub - 1) // sub
    k_in_e = first[:, None] + jnp.arange(tm, dtype=i32)[None, :]
    valid = k_in_e < counts[tile_expert][:, None]
    src = jnp.clip(start[tile_expert][:, None] + k_in_e, 0, n_assign - 1)
    row_token = jnp.where(valid, order[src] // MOE_TOP_K, 0).astype(i32)
    dest = (pad_start[sorted_e] + jnp.arange(n_assign, dtype=i32) - start[sorted_e]).astype(i32)
    _, pos = lax.sort_key_val(order, dest)
    return row_token, pos, tile_expert, n_used, n_sub.astype(i32)


def _moe_layer(x_packed, top_e, w_gu, b_gu, w_down, b_down, layer):
    tok_tiles, pos, tile_expert, n_used, n_sub = _route(top_e, MOE_TILE_M, MOE_SUB_M)
    ys = _moe_experts(x_packed, tok_tiles, tile_expert, n_used, n_sub,
                      w_gu, b_gu, w_down, b_down, layer)
    pos_tiles = pos.reshape(-1, COMBINE_TILE, MOE_TOP_K).transpose(0, 2, 1)
    return ys, pos_tiles.reshape(-1, MOE_TOP_K * COMBINE_TILE)


def _router_params(router_w, router_b):
    rw = jnp.pad(router_w, ((0, 0), (0, LANES - N_EXPERTS)))
    rw_hi = lax.reduce_precision(rw, exponent_bits=8, mantissa_bits=7)
    rw_lo = rw - rw_hi
    rb = jnp.pad(router_b, (0, LANES - N_EXPERTS), constant_values=MASK_VALUE)
    return jnp.concatenate([rw_hi, rw_lo], axis=1).astype(BF16), rb[None, :]


def kernel(x, ln_mix_g, ln_mix_b, ln_ffn_g, ln_ffn_b, w_in_ab, w_out_ab,
           mla_w_dq, mla_q_norm, mla_w_uq, mla_w_dkv, mla_kv_norm, mla_w_ukv, mla_w_out,
           router_w, router_b, moe_w_gu, moe_b_gu, moe_w_down, moe_b_down):
    b, s, d = x.shape
    t = b * s
    xt = x.reshape(t, d)
    cos_h, sin_h = _rope_tables(s, HEAD_DIM, LANES)
    cos_r, sin_r = _rope_tables(s, MLA_ROPE_DIM, LANES)
    row2 = lambda v: v[None, :]

    for layer in range(DEPTH):
        i = layer // 2
        rw, rb = _router_params(router_w[layer], router_b[layer])
        if layer % 2 == 0:
            proj = _matmul(xt, w_in_ab[i].astype(BF16)).reshape(b, s, -1)
            o_a = _longnet(proj, cos_h, sin_h).reshape(t, -1)
            o_b = _moba(proj, cos_h, sin_h).reshape(t, -1)
            w_out = w_out_ab[i].astype(BF16)
            split = A_HEADS * HEAD_DIM
            outs, ws = [o_a, o_b], [w_out[:split], w_out[split:]]
        else:
            w_cat = jnp.concatenate(
                [mla_w_dq[i], mla_w_dkv[i][:, :MLA_KV_RANK],
                 _pad_rope_cols(mla_w_dkv[i][:, MLA_KV_RANK:])], axis=1).astype(BF16)
            cq, ckv, kr = _mla_down(xt, w_cat, row2(mla_q_norm[i]), row2(mla_kv_norm[i]))
            wq = mla_w_uq[i].reshape(MLA_Q_RANK, MLA_HEADS, MLA_NOPE_DIM + MLA_ROPE_DIM)
            wq = jnp.concatenate([wq[..., :MLA_NOPE_DIM],
                                  _pad_rope_cols(wq[..., MLA_NOPE_DIM:])], axis=-1)
            wq = wq.reshape(MLA_Q_RANK, MLA_HEADS * 2 * LANES).astype(BF16)
            o = _mla_attn(cq.reshape(b, s, -1), ckv.reshape(b, s, -1), kr.reshape(b, s, -1),
                          wq, mla_w_ukv[i].astype(BF16), cos_r, sin_r)
            outs, ws = [o.reshape(t, -1)], [mla_w_out[i].astype(BF16)]
        xt, x_packed, top_e, gates = _outproj_ln(outs, ws, xt, row2(ln_mix_g[layer]),
                                                 row2(ln_mix_b[layer]), rw, rb)
        ys, pos_tiles = _moe_layer(x_packed, top_e[:, :MOE_TOP_K], moe_w_gu, moe_b_gu,
                                   moe_w_down, moe_b_down, layer)
        xt = _combine_ln(pos_tiles, ys, gates, xt,
                         row2(ln_ffn_g[layer]), row2(ln_ffn_b[layer]))
    return xt.reshape(b, s, d)
```

```python
import functools

import jax
import jax.numpy as jnp
from jax import lax
from jax.experimental import pallas as pl
from jax.experimental.pallas import tpu as pltpu

F32 = jnp.float32
BF16 = jnp.bfloat16

DEPTH = 2
HEAD_DIM = 128
A_HEADS = 12
B_HEADS = 4
DILATED_CONFIGS = ((128, 1), (512, 4), (2048, 16))
MOBA_BLOCK = 256
MOBA_TOP_K = 3
ROPE_THETA = 10000.0
MLA_HEADS = 16
MLA_Q_RANK = 512
MLA_KV_RANK = 512
MLA_NOPE_DIM = 128
MLA_ROPE_DIM = 64
MLA_V_DIM = 128
N_EXPERTS = 32
MOE_TOP_K = 4
SWIGLU_LIMIT = 7.0
SWIGLU_ALPHA = 1.702
LN_EPS = 1e-5
RMS_EPS = 1e-6
ALPHA = (2.0 * DEPTH) ** 0.25

LANES = 128
MASK_VALUE = -1e30
VMEM_LIMIT = 56 * 1024 * 1024

MM_TILE_M = 1024
MM_TILE_N = 1024
LN_TILE_M = 256
ATT_TILE_Q = 512
LONGNET_GROUP = 8
MOE_TILE_M = 1024
MOE_SUB_M = 256
MOE_TILE_F = 512
MOE_TILE_N = 1024
COMBINE_TILE = 256
DMA_UNROLL = 16


def _params(*semantics):
    return pltpu.CompilerParams(dimension_semantics=semantics,
                                vmem_limit_bytes=VMEM_LIMIT)


def _dot(a, b):
    return jnp.dot(a, b, preferred_element_type=F32)


def _dot_nt(a, b):
    return lax.dot_general(a, b, (((1,), (1,)), ((), ())),
                           preferred_element_type=F32)


def _dot_nt_f32(a, b):
    return lax.dot_general(a, b, (((1,), (1,)), ((), ())),
                           precision=lax.Precision.HIGHEST,
                           preferred_element_type=F32)


def _exp_bf16(z):
    return jnp.exp(z.astype(BF16))


def _with_ones(v):
    return jnp.concatenate([v, jnp.ones(v.shape, v.dtype)], axis=-1)


def _rope(x, cos, sin):
    return x * cos + pltpu.roll(x, LANES // 2, 1) * sin


def _mm_kernel(x_ref, w_ref, o_ref, xb_ref):
    @pl.when(pl.program_id(1) == 0)
    def _():
        xb_ref[...] = x_ref[...].astype(BF16)
    o_ref[...] = _dot(xb_ref[...], w_ref[...])


def _matmul(x, w):
    m, k = x.shape
    n = w.shape[1]
    tm, tn = min(MM_TILE_M, m), min(MM_TILE_N, n)
    return pl.pallas_call(
        _mm_kernel,
        out_shape=jax.ShapeDtypeStruct((m, n), F32),
        grid=(m // tm, n // tn),
        in_specs=[pl.BlockSpec((tm, k), lambda i, j: (i, 0)),
                  pl.BlockSpec((k, tn), lambda i, j: (0, j))],
        out_specs=pl.BlockSpec((tm, tn), lambda i, j: (i, j)),
        scratch_shapes=[pltpu.VMEM((tm, k), BF16)],
        compiler_params=_params("parallel", "arbitrary"),
        name="proj_matmul",
    )(x, w)


def _longnet_kernel(q_ref, k_ref, v_ref, cos_ref, sin_ref, o_ref,
                    qs, ks, vs, oc, lc, *, seq):
    band = DILATED_CONFIGS[0][0] // DILATED_CONFIGS[0][1]
    cos, sin = cos_ref[...], sin_ref[...]
    qs[...] = _rope(q_ref[0], cos, sin) * (HEAD_DIM ** -0.5)
    ks[...] = _rope(k_ref[0], cos, sin)
    vs[...] = v_ref[0]

    def band_mask(g, with_prev):
        width = 2 * band if with_prev else band
        qi = lax.broadcasted_iota(jnp.int32, (g, band, width), 1)
        kj = lax.broadcasted_iota(jnp.int32, (g, band, width), 2)
        if with_prev:
            return jnp.logical_and(kj >= qi, kj <= qi + band)
        return kj <= qi

    def rows(dil, start, size):
        return pl.ds(start, size) if dil == 1 else pl.ds(start, size, stride=dil)

    first_blocks, later_blocks = [], []
    for c, (window, dil) in enumerate(DILATED_CONFIGS):
        assert window // dil == band and seq % (dil * band) == 0
        for r in range(dil):
            for n in range(seq // (dil * band)):
                (later_blocks if n else first_blocks).append((c, dil, r + dil * band * n))

    def attend(group, with_prev):
        def stacked(ref, back, size):
            return jnp.stack([ref[rows(d, st - back * d * band, size), :]
                              for _, d, st in group]).astype(BF16)
        qb = stacked(qs, 0, band)
        kb = stacked(ks, 1, 2 * band) if with_prev else stacked(ks, 0, band)
        vb = stacked(vs, 1, 2 * band) if with_prev else stacked(vs, 0, band)
        s = lax.dot_general(qb, kb, (((2,), (2,)), ((0,), (0,))),
                            preferred_element_type=F32)
        s = jnp.where(band_mask(len(group), with_prev), s, MASK_VALUE)
        m = jnp.max(s, axis=-1, keepdims=True)
        acc = lax.dot_general(_exp_bf16(s - m), _with_ones(vb), (((2,), (1,)), ((0,), (0,))),
                              preferred_element_type=F32)
        l = acc[:, :, HEAD_DIM:]
        o = acc[:, :, :HEAD_DIM] / l
        lse = m + jnp.log(l[:, :, :1])
        for g, (c, d, st) in enumerate(group):
            oc[c, rows(d, st, band), :] = o[g]
            lc[c, rows(d, st, band), :] = jnp.broadcast_to(lse[g], (band, LANES))

    for blocks, with_prev in ((first_blocks, False), (later_blocks, True)):
        for i in range(0, len(blocks), LONGNET_GROUP):
            attend(blocks[i:i + LONGNET_GROUP], with_prev)

    lses = [lc[c] for c in range(len(DILATED_CONFIGS))]
    top = functools.reduce(jnp.maximum, lses)
    es = [jnp.exp(v - top) for v in lses]
    num = sum(e * oc[c] for c, e in enumerate(es))
    o_ref[0] = num / sum(es)


def _longnet(proj, cos, sin):
    b, s, _ = proj.shape
    n_cfg = len(DILATED_CONFIGS)
    blk = lambda off: pl.BlockSpec((1, s, HEAD_DIM), lambda i, h: (i, 0, off + h))
    tab = pl.BlockSpec((s, HEAD_DIM), lambda i, h: (0, 0))
    return pl.pallas_call(
        functools.partial(_longnet_kernel, seq=s),
        out_shape=jax.ShapeDtypeStruct((b, s, A_HEADS * HEAD_DIM), F32),
        grid=(b, A_HEADS),
        in_specs=[blk(0), blk(A_HEADS), blk(2 * A_HEADS), tab, tab],
        out_specs=pl.BlockSpec((1, s, HEAD_DIM), lambda i, h: (i, 0, h)),
        scratch_shapes=[pltpu.VMEM((s, HEAD_DIM), F32),
                        pltpu.VMEM((s, HEAD_DIM), F32),
                        pltpu.VMEM((s, HEAD_DIM), F32),
                        pltpu.VMEM((n_cfg, s, HEAD_DIM), F32),
                        pltpu.VMEM((n_cfg, s, LANES), F32)],
        compiler_params=_params("parallel", "parallel"),
        name="longnet_attention",
    )(proj, proj, proj, cos, sin)


def _softmax_blocks(qb, k_ref, v1_ref, blocks, tile):
    scores = []
    m = None
    for start, mask in blocks:
        s = _dot_nt(qb, k_ref[pl.ds(start, tile), :])
        if mask is not None:
            s = jnp.where(mask, s, MASK_VALUE)
        scores.append(s)
        bm = jnp.max(s, axis=-1, keepdims=True)
        m = bm if m is None else jnp.maximum(m, bm)
    acc = None
    for (start, _), s in zip(blocks, scores):
        pv = _dot(_exp_bf16(s - m), v1_ref[pl.ds(start, tile), :])
        acc = pv if acc is None else acc + pv
    half = acc.shape[1] // 2
    return acc[:, :half] / acc[:, half:]


def _moba_kernel(q_ref, k_ref, v_ref, cos_ref, sin_ref, o_ref, qs, ks, vs, *, seq):
    blk = MOBA_BLOCK
    nblk = seq // blk
    cos, sin = cos_ref[...], sin_ref[...]
    q = _rope(q_ref[0], cos, sin)
    k = _rope(k_ref[0], cos, sin)
    qs[...] = (q * (HEAD_DIM ** -0.5)).astype(BF16)
    ks[...] = k.astype(BF16)
    vs[...] = _with_ones(v_ref[0].astype(BF16))
    k_mean = jnp.concatenate(
        [jnp.mean(k[n * blk:(n + 1) * blk], axis=0, keepdims=True) for n in range(nblk)],
        axis=0)
    gate = _dot_nt_f32(q, k_mean)
    row = lax.broadcasted_iota(jnp.int32, (blk, blk), 0)
    col = lax.broadcasted_iota(jnp.int32, (blk, blk), 1)
    causal = col <= row
    n_top = min(MOBA_TOP_K, nblk)

    for i in range(nblk):
        g = gate[i * blk:(i + 1) * blk, :]
        blocks = [(i * blk, causal)]
        for n in range(i):
            if i <= n_top:
                blocks.append((n * blk, None))
                continue
            gn = g[:, n:n + 1]
            rank = jnp.zeros((blk, 1), jnp.int32)
            for o in range(i):
                if o == n:
                    continue
                go = g[:, o:o + 1]
                ahead = (go > gn) | (go == gn) if o < n else (go > gn)
                rank = rank + ahead.astype(jnp.int32)
            blocks.append((n * blk, rank < n_top))
        o_ref[0, pl.ds(i * blk, blk), :] = _softmax_blocks(
            qs[pl.ds(i * blk, blk), :], ks, vs, blocks, blk)


def _moba(proj, cos, sin):
    b, s, _ = proj.shape
    base = 3 * A_HEADS
    blk = lambda off: pl.BlockSpec((1, s, HEAD_DIM), lambda i, h: (i, 0, base + off + h))
    tab = pl.BlockSpec((s, HEAD_DIM), lambda i, h: (0, 0))
    return pl.pallas_call(
        functools.partial(_moba_kernel, seq=s),
        out_shape=jax.ShapeDtypeStruct((b, s, B_HEADS * HEAD_DIM), F32),
        grid=(b, B_HEADS),
        in_specs=[blk(0), blk(B_HEADS), blk(2 * B_HEADS), tab, tab],
        out_specs=pl.BlockSpec((1, s, HEAD_DIM), lambda i, h: (i, 0, h)),
        scratch_shapes=[pltpu.VMEM((s, HEAD_DIM), BF16)] * 2
        + [pltpu.VMEM((s, 2 * HEAD_DIM), BF16)],
        compiler_params=_params("parallel", "parallel"),
        name="moba_attention",
    )(proj, proj, proj, cos, sin)


def _layer_norm(z, g, b):
    mu = jnp.mean(z, axis=-1, keepdims=True)
    zc = z - mu
    var = jnp.mean(zc * zc, axis=-1, keepdims=True)
    return zc * lax.rsqrt(var + LN_EPS) * g + b


def _route_topk(y, rw_ref, rb_ref, e_ref, g_ref):
    y_hi = y.astype(BF16)
    y_lo = (y - y_hi.astype(F32)).astype(BF16)
    hi_both = _dot(y_hi, rw_ref[...])
    lo_hi = _dot(y_lo, rw_ref[:, :LANES])
    logits = hi_both[:, :LANES] + (hi_both[:, LANES:] + lo_hi) + rb_ref[...]
    lane = lax.broadcasted_iota(jnp.int32, logits.shape, 1)
    vals, idxs = [], []
    for _ in range(MOE_TOP_K):
        v = jnp.max(logits, axis=-1, keepdims=True)
        ix = jnp.min(jnp.where(logits == v, lane, LANES), axis=-1, keepdims=True)
        vals.append(v)
        idxs.append(ix)
        logits = jnp.where(lane == ix, -jnp.inf, logits)
    exps = [jnp.exp(v - vals[0]) for v in vals]
    denom = sum(exps)
    e_out = jnp.zeros(lane.shape, jnp.int32)
    g_out = jnp.zeros(lane.shape, F32)
    for kk in range(MOE_TOP_K):
        e_out = jnp.where(lane == kk, idxs[kk], e_out)
        g_out = jnp.where(lane == kk, exps[kk] / denom, g_out)
    e_ref[...] = e_out
    g_ref[...] = g_out


_HI16 = 0xFFFF0000


def _pack_bf16_halves(y):
    half = y.shape[1] // 2
    bits = lambda v: lax.bitcast_convert_type(v.astype(BF16).astype(F32), jnp.uint32)
    return (bits(y[:, :half]) >> 16) | (bits(y[:, half:]) & jnp.uint32(_HI16))


def _unpack_bf16_halves(pk):
    lo = lax.bitcast_convert_type(pk << 16, F32).astype(BF16)
    hi = lax.bitcast_convert_type(pk & jnp.uint32(_HI16), F32).astype(BF16)
    return lo, hi


def _outproj_ln_kernel(*refs, n_in):
    o_refs, w_refs = refs[:n_in], refs[n_in:2 * n_in]
    x_ref, g_ref, b_ref, rw_ref, rb_ref, y_ref, pk_ref, e_ref, gt_ref = refs[2 * n_in:]
    h = sum(_dot(o[...].astype(BF16), w[...]) for o, w in zip(o_refs, w_refs))
    y = _layer_norm(ALPHA * x_ref[...] + h, g_ref[...], b_ref[...])
    y_ref[...] = y
    pk_ref[...] = _pack_bf16_halves(y)
    _route_topk(y, rw_ref, rb_ref, e_ref, gt_ref)


def _outproj_ln(outs, ws, x, g, b, rw, rb):
    t, d = x.shape
    tm = LN_TILE_M
    n_in = len(outs)
    row = lambda width: pl.BlockSpec((tm, width), lambda i: (i, 0))
    full = lambda a: pl.BlockSpec(a.shape, lambda i: (0,) * a.ndim)
    return pl.pallas_call(
        functools.partial(_outproj_ln_kernel, n_in=n_in),
        out_shape=(jax.ShapeDtypeStruct((t, d), F32),
                   jax.ShapeDtypeStruct((t, d // 2), jnp.uint32),
                   jax.ShapeDtypeStruct((t, LANES), jnp.int32),
                   jax.ShapeDtypeStruct((t, LANES), F32)),
        grid=(t // tm,),
        in_specs=([row(o.shape[1]) for o in outs] + [full(w) for w in ws]
                  + [row(d), full(g), full(b), full(rw), full(rb)]),
        out_specs=(row(d), row(d // 2), row(LANES), row(LANES)),
        compiler_params=_params("parallel"),
        name="outproj_layernorm_router",
    )(*outs, *ws, x, g, b, rw, rb)


def _rms_norm(x, g):
    return x * lax.rsqrt(jnp.mean(x * x, axis=-1, keepdims=True) + RMS_EPS) * g


def _mla_down_kernel(x_ref, w_ref, qn_ref, kvn_ref, cq_ref, ckv_ref, kr_ref):
    a = _dot(x_ref[...].astype(BF16), w_ref[...])
    cq_ref[...] = _rms_norm(a[:, :MLA_Q_RANK], qn_ref[...]).astype(BF16)
    ckv_ref[...] = _rms_norm(a[:, MLA_Q_RANK:MLA_Q_RANK + MLA_KV_RANK],
                             kvn_ref[...]).astype(BF16)
    kr_ref[...] = a[:, MLA_Q_RANK + MLA_KV_RANK:]


def _mla_down(x, w_cat, q_norm, kv_norm):
    t, d = x.shape
    tm = MM_TILE_M
    row = lambda width: pl.BlockSpec((tm, width), lambda i: (i, 0))
    full = lambda a: pl.BlockSpec(a.shape, lambda i: (0,) * a.ndim)
    return pl.pallas_call(
        _mla_down_kernel,
        out_shape=(jax.ShapeDtypeStruct((t, MLA_Q_RANK), BF16),
                   jax.ShapeDtypeStruct((t, MLA_KV_RANK), BF16),
                   jax.ShapeDtypeStruct((t, LANES), F32)),
        grid=(t // tm,),
        in_specs=[row(d), full(w_cat), full(q_norm), full(kv_norm)],
        out_specs=(row(MLA_Q_RANK), row(MLA_KV_RANK), row(LANES)),
        compiler_params=_params("parallel"),
        name="mla_down_rmsnorm",
    )(x, w_cat, q_norm, kv_norm)


def _mla_attn_kernel(cq_ref, ckv_ref, kr_ref, wq_ref, wkv_ref, cos_ref, sin_ref,
                     o_ref, qs, ks, vs, *, seq):
    tile = ATT_TILE_Q
    scale = (MLA_NOPE_DIM + MLA_ROPE_DIM) ** -0.5
    cos, sin = cos_ref[...], sin_ref[...]
    q = _dot(cq_ref[0], wq_ref[...])
    kv = _dot(ckv_ref[0], wkv_ref[...])
    qs[:, :LANES] = (q[:, :LANES] * scale).astype(BF16)
    qs[:, LANES:] = (_rope(q[:, LANES:], cos, sin) * scale).astype(BF16)
    ks[:, :LANES] = kv[:, :MLA_NOPE_DIM].astype(BF16)
    ks[:, LANES:] = _rope(kr_ref[0], cos, sin).astype(BF16)
    vs[...] = _with_ones(kv[:, MLA_NOPE_DIM:].astype(BF16))
    row = lax.broadcasted_iota(jnp.int32, (tile, tile), 0)
    col = lax.broadcasted_iota(jnp.int32, (tile, tile), 1)
    causal = col <= row
    for i in range(seq // tile):
        past = i * tile
        keys = pl.ds(0, past + tile)
        s = _dot_nt(qs[pl.ds(past, tile), :], ks[keys, :])
        s_diag = jnp.where(causal, s[:, past:], MASK_VALUE)
        m = jnp.max(s_diag, axis=-1, keepdims=True)
        if i:
            m = jnp.maximum(m, jnp.max(s[:, :past], axis=-1, keepdims=True))
            p = jnp.concatenate([_exp_bf16(s[:, :past] - m), _exp_bf16(s_diag - m)], axis=1)
        else:
            p = _exp_bf16(s_diag - m)
        acc = _dot(p, vs[keys, :])
        o_ref[0, pl.ds(past, tile), :] = acc[:, :MLA_V_DIM] / acc[:, MLA_V_DIM:]


def _mla_attn(cq, ckv, kr, wq, wkv, cos, sin):
    b, s, _ = cq.shape
    per_b = lambda width: pl.BlockSpec((1, s, width), lambda i, h: (i, 0, 0))
    head_w = lambda a: pl.BlockSpec((a.shape[0], 2 * LANES), lambda i, h: (0, h))
    tab = pl.BlockSpec((s, LANES), lambda i, h: (0, 0))
    return pl.pallas_call(
        functools.partial(_mla_attn_kernel, seq=s),
        out_shape=jax.ShapeDtypeStruct((b, s, MLA_HEADS * MLA_V_DIM), F32),
        grid=(b, MLA_HEADS),
        in_specs=[per_b(MLA_Q_RANK), per_b(MLA_KV_RANK), per_b(LANES),
                  head_w(wq), head_w(wkv), tab, tab],
        out_specs=pl.BlockSpec((1, s, MLA_V_DIM), lambda i, h: (i, 0, h)),
        scratch_shapes=[pltpu.VMEM((s, 2 * LANES), BF16),
                        pltpu.VMEM((s, 2 * LANES), BF16),
                        pltpu.VMEM((s, 2 * MLA_V_DIM), BF16)],
        compiler_params=_params("parallel", "arbitrary"),
        name="mla_attention",
    )(cq, ckv, kr, wq, wkv, cos, sin)


def _issue_row_gather(idx_of, src_hbm, n, sem_rows, dst_of, alternate=False):
    def issue(c, carry):
        base = pl.multiple_of(c * DMA_UNROLL, DMA_UNROLL)
        for u in range(DMA_UNROLL):
            pltpu.make_async_copy(src_hbm.at[pl.ds(idx_of(base + u), 1)],
                                  dst_of(base, u), sem_rows).start(
                                      priority=u % 2 if alternate else 0)
        return carry
    lax.fori_loop(0, n // DMA_UNROLL, issue, 0)


def _for_row_counts(ns, tm, fn):
    for n in range(1, tm // MOE_SUB_M + 1):
        pl.when(ns == n)(functools.partial(fn, n * MOE_SUB_M))


def _moe_gu_kernel(te_ref, nu_ref, ns_ref, tok_hbm, x_hbm, wgu_ref, bgu_ref, cmp_ref,
                   o_ref, tok_smem, xbuf, xb, wgu_b, sem_idx, sem_rows):
    t, j = pl.program_id(0), pl.program_id(1)
    _, tm, half = xbuf.shape
    two_f = wgu_b.shape[1]
    sel = cmp_ref.shape[0]
    used = t < nu_ref[0]
    slot = t % 2

    def idx_copy(tile, sl):
        return pltpu.make_async_copy(
            tok_hbm.at[tile], tok_smem.at[pl.ds(pl.multiple_of(sl * tm, tm), tm)],
            sem_idx.at[sl])

    def start_gather(tile, sl):
        _issue_row_gather(lambda a: tok_smem[sl * tm + a], x_hbm,
                          ns_ref[tile] * MOE_SUB_M, sem_rows.at[sl],
                          lambda base, u: xbuf.at[sl, pl.ds(base + u, 1)])

    @pl.when(jnp.logical_and(used, j == 0))
    def _():
        @pl.when(t == 0)
        def _():
            idx_copy(0, 0).start()
            idx_copy(0, 0).wait()
            start_gather(0, 0)

            @pl.when(nu_ref[0] > 1)
            def _():
                idx_copy(1, 1).start()

        def land(rows):
            r = pl.ds(0, rows)
            pltpu.make_async_copy(x_hbm.at[r], xbuf.at[slot, r], sem_rows.at[slot]).wait()
            lo, hi = _unpack_bf16_halves(xbuf[slot, r, :])
            xb[r, :half] = lo
            xb[r, half:] = hi

        _for_row_counts(ns_ref[t], tm, land)

        @pl.when(t + 1 < nu_ref[0])
        def _():
            idx_copy(t + 1, 1 - slot).wait()
            start_gather(t + 1, 1 - slot)

            @pl.when(t + 2 < nu_ref[0])
            def _():
                idx_copy(t + 2, slot).start()

    @pl.when(jnp.logical_not(used))
    def _():
        o_ref[...] = jnp.zeros(o_ref.shape, o_ref.dtype)

    @pl.when(used)
    def _():
        def compute(rows):
            r = pl.ds(0, rows)
            x = xb[r, :]
            is_glu = (lax.broadcasted_iota(jnp.int32, (rows, sel), 1) & 1) == 0
            for c in range(0, two_f, sel):
                wgu_b[:, c:c + sel] = wgu_ref[0, 0, :, c:c + sel].astype(BF16)
                h = _dot(x, wgu_b[:, c:c + sel]) + bgu_ref[0, 0, :, c:c + sel]
                glu = jnp.minimum(h, SWIGLU_LIMIT)
                lin = jnp.clip(h, -SWIGLU_LIMIT, SWIGLU_LIMIT) + 1.0
                lin = jnp.concatenate(
                    [pltpu.roll(lin[:, v:v + LANES], LANES - 1, 1)
                     for v in range(0, sel, LANES)], axis=1)
                act = jnp.where(is_glu, glu * jax.nn.sigmoid(SWIGLU_ALPHA * glu) * lin, 0.0)
                o_ref[r, c // 2:(c + sel) // 2] = _dot(
                    act.astype(BF16), cmp_ref[...]).astype(o_ref.dtype)
            if rows < tm:
                o_ref[pl.ds(rows, tm - rows), :] = jnp.zeros(
                    (tm - rows, o_ref.shape[1]), o_ref.dtype)

        _for_row_counts(ns_ref[t], tm, compute)


def _moe_down_kernel(te_ref, nu_ref, ns_ref, a_ref, wd_ref, bd_ref, o_ref, wd_b):
    t = pl.program_id(1)
    tm = a_ref.shape[0]

    @pl.when(t >= nu_ref[0])
    def _():
        o_ref[...] = jnp.zeros(o_ref.shape, F32)

    @pl.when(t < nu_ref[0])
    def _():
        @pl.when(jnp.logical_or(t == 0, te_ref[t] != te_ref[jnp.maximum(t - 1, 0)]))
        def _():
            wd_b[...] = wd_ref[0, 0].astype(BF16)

        def compute(rows):
            r = pl.ds(0, rows)
            o_ref[r, :] = _dot(a_ref[r, :], wd_b[...]) + bd_ref[0, 0]
            if rows < tm:
                o_ref[pl.ds(rows, tm - rows), :] = jnp.zeros(
                    (tm - rows, o_ref.shape[1]), F32)

        _for_row_counts(ns_ref[t], tm, compute)


def _moe_experts(x_packed, tok_tiles, tile_expert, n_used, n_sub, w_gu, b_gu, w_down,
                 b_down, layer):
    n_tiles, tm = tok_tiles.shape
    half = x_packed.shape[1]
    d = 2 * half
    ff = w_down.shape[2]
    tf, tn = MOE_TILE_F, MOE_TILE_N
    n_f, n_n = ff // tf, d // tn
    sel = 4 * LANES
    compact = (jnp.arange(sel)[:, None] == 2 * jnp.arange(sel // 2)[None, :]).astype(BF16)

    def tile(i, nu):
        return jnp.minimum(i, nu[0] - 1)

    def chunk(i, j, nu, last):
        return jnp.where(i < nu[0], j, last)

    act = pl.pallas_call(
        _moe_gu_kernel,
        out_shape=jax.ShapeDtypeStruct((n_tiles * tm, ff), BF16),
        grid_spec=pltpu.PrefetchScalarGridSpec(
            num_scalar_prefetch=3,
            grid=(n_tiles, n_f),
            in_specs=[
                pl.BlockSpec(memory_space=pl.ANY),
                pl.BlockSpec(memory_space=pl.ANY),
                pl.BlockSpec((1, 1, d, 2 * tf), lambda i, j, te, nu, ns:
                             (layer, te[tile(i, nu)], 0, chunk(i, j, nu, n_f - 1))),
                pl.BlockSpec((1, 1, 1, 2 * tf), lambda i, j, te, nu, ns:
                             (layer, te[tile(i, nu)], 0, chunk(i, j, nu, n_f - 1))),
                pl.BlockSpec((sel, sel // 2), lambda i, j, te, nu, ns: (0, 0)),
            ],
            out_specs=pl.BlockSpec((tm, tf), lambda i, j, te, nu, ns: (i, j)),
            scratch_shapes=[pltpu.SMEM((2 * tm,), jnp.int32),
                            pltpu.VMEM((2, tm, half), jnp.uint32),
                            pltpu.VMEM((tm, d), BF16),
                            pltpu.VMEM((d, 2 * tf), BF16),
                            pltpu.SemaphoreType.DMA((2,)),
                            pltpu.SemaphoreType.DMA((2,))]),
        compiler_params=_params("arbitrary", "arbitrary"),
        name="moe_gate_up",
    )(tile_expert, n_used, n_sub, tok_tiles, x_packed, w_gu, b_gu[:, :, None, :], compact)

    return pl.pallas_call(
        _moe_down_kernel,
        out_shape=jax.ShapeDtypeStruct((n_tiles * tm, d), F32),
        grid_spec=pltpu.PrefetchScalarGridSpec(
            num_scalar_prefetch=3,
            grid=(n_n, n_tiles),
            in_specs=[
                pl.BlockSpec((tm, ff), lambda j, i, te, nu, ns: (tile(i, nu), 0)),
                pl.BlockSpec((1, 1, ff, tn), lambda j, i, te, nu, ns:
                             (layer, te[tile(i, nu)], 0, j)),
                pl.BlockSpec((1, 1, 1, tn), lambda j, i, te, nu, ns:
                             (layer, te[tile(i, nu)], 0, j)),
            ],
            out_specs=pl.BlockSpec((tm, tn), lambda j, i, te, nu, ns: (i, j)),
            scratch_shapes=[pltpu.VMEM((ff, tn), BF16)]),
        compiler_params=_params("arbitrary", "arbitrary"),
        name="moe_down",
    )(tile_expert, n_used, n_sub, act, w_down, b_down[:, :, None, :])


def _combine_kernel(pos_hbm, ys_hbm, gates_ref, x_ref, g_ref, b_ref, y_ref,
                    pos_smem, buf, sem_idx, sem_rows):
    i = pl.program_id(0)
    n = pl.num_programs(0)
    _, k, tc, _ = buf.shape
    slot = i % 2

    def idx_copy(tile, sl):
        return pltpu.make_async_copy(
            pos_hbm.at[tile], pos_smem.at[pl.ds(pl.multiple_of(sl * k * tc, k * tc), k * tc)],
            sem_idx.at[sl])

    def start_gather(sl):
        _issue_row_gather(
            lambda a: pos_smem[sl * k * tc + a], ys_hbm, k * tc, sem_rows.at[sl],
            lambda base, u: buf.at[sl, base // tc,
                                   pl.ds(pl.multiple_of(base % tc, DMA_UNROLL) + u, 1)],
            alternate=True)

    @pl.when(i == 0)
    def _():
        idx_copy(0, 0).start()
        idx_copy(0, 0).wait()
        start_gather(0)

        @pl.when(n > 1)
        def _():
            idx_copy(1, 1).start()

    for kk in range(k):
        pltpu.make_async_copy(ys_hbm.at[pl.ds(0, tc)], buf.at[slot, kk],
                              sem_rows.at[slot]).wait()

    @pl.when(i + 1 < n)
    def _():
        idx_copy(i + 1, 1 - slot).wait()
        start_gather(1 - slot)

        @pl.when(i + 2 < n)
        def _():
            idx_copy(i + 2, slot).start()

    gates = gates_ref[...]
    h = sum(gates[:, kk:kk + 1] * buf[slot, kk] for kk in range(k))
    y_ref[...] = _layer_norm(ALPHA * x_ref[...] + h, g_ref[...], b_ref[...])


def _combine_ln(pos_tiles, ys, gates, x, g, b):
    t, d = x.shape
    tc = COMBINE_TILE
    row = lambda width: pl.BlockSpec((tc, width), lambda i: (i, 0))
    full = lambda a: pl.BlockSpec(a.shape, lambda i: (0,) * a.ndim)
    anyspec = pl.BlockSpec(memory_space=pl.ANY)
    return pl.pallas_call(
        _combine_kernel,
        out_shape=jax.ShapeDtypeStruct((t, d), F32),
        grid=(t // tc,),
        in_specs=[anyspec, anyspec, row(LANES), row(d), full(g), full(b)],
        out_specs=row(d),
        scratch_shapes=[pltpu.SMEM((2 * MOE_TOP_K * tc,), jnp.int32),
                        pltpu.VMEM((2, MOE_TOP_K, tc, d), F32),
                        pltpu.SemaphoreType.DMA((2,)),
                        pltpu.SemaphoreType.DMA((2,))],
        compiler_params=_params("arbitrary"),
        name="moe_combine_layernorm",
    )(pos_tiles, ys, gates, x, g, b)


def _rope_tables(seq, dim, slab):
    half = dim // 2
    pos = jnp.arange(seq, dtype=F32)
    inv = 1.0 / (ROPE_THETA ** (jnp.arange(0, dim, 2, dtype=F32) / dim))
    ang = pos[:, None] * inv[None, :]
    pad = jnp.zeros((seq, slab // 2 - half), F32)
    cos = jnp.concatenate([jnp.cos(ang), pad, jnp.cos(ang), pad], axis=1)
    sin = jnp.concatenate([-jnp.sin(ang), pad, jnp.sin(ang), pad], axis=1)
    return cos, sin


def _pad_rope_cols(w):
    half = MLA_ROPE_DIM // 2
    z = jnp.zeros(w.shape[:-1] + (LANES // 2 - half,), w.dtype)
    return jnp.concatenate([w[..., :half], z, w[..., half:], z], axis=-1)


def _route(top_e, tm, sub):
    t = top_e.shape[0]
    i32 = jnp.int32
    flat_e = top_e.reshape(-1)
    n_assign = t * MOE_TOP_K
    n_tiles = n_assign // tm + N_EXPERTS
    order = jnp.argsort(flat_e).astype(i32)
    sorted_e = flat_e[order]
    counts = jnp.sum(flat_e[:, None] == jnp.arange(N_EXPERTS, dtype=i32)[None, :],
                     axis=0, dtype=i32)
    padded = (counts + tm - 1) // tm * tm
    start = jnp.cumsum(counts) - counts
    pad_end = jnp.cumsum(padded)
    pad_start = pad_end - padded
    tile_start = jnp.arange(n_tiles, dtype=i32) * tm
    tile_expert = jnp.minimum(
        jnp.sum(pad_end[None, :] <= tile_start[:, None], axis=1, dtype=i32), N_EXPERTS - 1)
    n_used = (pad_end[-1:] // tm).astype(i32)
    first = tile_start - pad_start[tile_expert]
    tile_rows = jnp.clip(counts[tile_expert] - first, 0, tm)
    n_sub = (tile_rows + sub - 1) // sub
    k_in_e = first[:, None] + jnp.arange(tm, dtype=i32)[None, :]
    valid = k_in_e < counts[tile_expert][:, None]
    src = jnp.clip(start[tile_expert][:, None] + k_in_e, 0, n_assign - 1)
    row_token = jnp.where(valid, order[src] // MOE_TOP_K, 0).astype(i32)
    dest = (pad_start[sorted_e] + jnp.arange(n_assign, dtype=i32) - start[sorted_e]).astype(i32)
    _, pos = lax.sort_key_val(order, dest)
    return row_token, pos, tile_expert, n_used, n_sub.astype(i32)


def _moe_layer(x_packed, top_e, w_gu, b_gu, w_down, b_down, layer):
    tok_tiles, pos, tile_expert, n_used, n_sub = _route(top_e, MOE_TILE_M, MOE_SUB_M)
    ys = _moe_experts(x_packed, tok_tiles, tile_expert, n_used, n_sub,
                      w_gu, b_gu, w_down, b_down, layer)
    pos_tiles = pos.reshape(-1, COMBINE_TILE, MOE_TOP_K).transpose(0, 2, 1)
    return ys, pos_tiles.reshape(-1, MOE_TOP_K * COMBINE_TILE)


def _router_params(router_w, router_b):
    rw = jnp.pad(router_w, ((0, 0), (0, LANES - N_EXPERTS)))
    rw_hi = lax.reduce_precision(rw, exponent_bits=8, mantissa_bits=7)
    rw_lo = rw - rw_hi
    rb = jnp.pad(router_b, (0, LANES - N_EXPERTS), constant_values=MASK_VALUE)
    return jnp.concatenate([rw_hi, rw_lo], axis=1).astype(BF16), rb[None, :]


def kernel(x, ln_mix_g, ln_mix_b, ln_ffn_g, ln_ffn_b, w_in_ab, w_out_ab,
           mla_w_dq, mla_q_norm, mla_w_uq, mla_w_dkv, mla_kv_norm, mla_w_ukv, mla_w_out,
           router_w, router_b, moe_w_gu, moe_b_gu, moe_w_down, moe_b_down):
    b, s, d = x.shape
    t = b * s
    xt = x.reshape(t, d)
    cos_h, sin_h = _rope_tables(s, HEAD_DIM, LANES)
    cos_r, sin_r = _rope_tables(s, MLA_ROPE_DIM, LANES)
    row2 = lambda v: v[None, :]

    for layer in range(DEPTH):
        i = layer // 2
        rw, rb = _router_params(router_w[layer], router_b[layer])
        if layer % 2 == 0:
            proj = _matmul(xt, w_in_ab[i].astype(BF16)).reshape(b, s, -1)
            o_a = _longnet(proj, cos_h, sin_h).reshape(t, -1)
            o_b = _moba(proj, cos_h, sin_h).reshape(t, -1)
            w_out = w_out_ab[i].astype(BF16)
            split = A_HEADS * HEAD_DIM
            outs, ws = [o_a, o_b], [w_out[:split], w_out[split:]]
        else:
            w_cat = jnp.concatenate(
                [mla_w_dq[i], mla_w_dkv[i][:, :MLA_KV_RANK],
                 _pad_rope_cols(mla_w_dkv[i][:, MLA_KV_RANK:])], axis=1).astype(BF16)
            cq, ckv, kr = _mla_down(xt, w_cat, row2(mla_q_norm[i]), row2(mla_kv_norm[i]))
            wq = mla_w_uq[i].reshape(MLA_Q_RANK, MLA_HEADS, MLA_NOPE_DIM + MLA_ROPE_DIM)
            wq = jnp.concatenate([wq[..., :MLA_NOPE_DIM],
                                  _pad_rope_cols(wq[..., MLA_NOPE_DIM:])], axis=-1)
            wq = wq.reshape(MLA_Q_RANK, MLA_HEADS * 2 * LANES).astype(BF16)
            o = _mla_attn(cq.reshape(b, s, -1), ckv.reshape(b, s, -1), kr.reshape(b, s, -1),
                          wq, mla_w_ukv[i].astype(BF16), cos_r, sin_r)
            outs, ws = [o.reshape(t, -1)], [mla_w_out[i].astype(BF16)]
        xt, x_packed, top_e, gates = _outproj_ln(outs, ws, xt, row2(ln_mix_g[layer]),
                                                 row2(ln_mix_b[layer]), rw, rb)
        ys, pos_tiles = _moe_layer(x_packed, top_e[:, :MOE_TOP_K], moe_w_gu, moe_b_gu,
                                   moe_w_down, moe_b_down, layer)
        xt = _combine_ln(pos_tiles, ys, gates, xt,
                         row2(ln_ffn_g[layer]), row2(ln_ffn_b[layer]))
    return xt.reshape(b, s, d)
```

```python
import functools

import jax
import jax.numpy as jnp
from jax import lax
from jax.experimental import pallas as pl
from jax.experimental.pallas import tpu as pltpu

F32 = jnp.float32
BF16 = jnp.bfloat16

DEPTH = 2
HEAD_DIM = 128
A_HEADS = 12
B_HEADS = 4
DILATED_CONFIGS = ((128, 1), (512, 4), (2048, 16))
MOBA_BLOCK = 256
MOBA_TOP_K = 3
ROPE_THETA = 10000.0
MLA_HEADS = 16
MLA_Q_RANK = 512
MLA_KV_RANK = 512
MLA_NOPE_DIM = 128
MLA_ROPE_DIM = 64
MLA_V_DIM = 128
N_EXPERTS = 32
MOE_TOP_K = 4
SWIGLU_LIMIT = 7.0
SWIGLU_ALPHA = 1.702
LN_EPS = 1e-5
RMS_EPS = 1e-6
ALPHA = (2.0 * DEPTH) ** 0.25

LANES = 128
MASK_VALUE = -1e30
VMEM_LIMIT = 56 * 1024 * 1024

MM_TILE_M = 1024
MM_TILE_N = 1024
LN_TILE_M = 256
ATT_TILE_Q = 1024
LONGNET_GROUP = 16
MOE_TILE_M = 1024
MOE_SUB_M = 256
MOE_TILE_F = 512
MOE_TILE_N = 1024
COMBINE_TILE = 256
DMA_UNROLL = 16


def _params(*semantics):
    return pltpu.CompilerParams(dimension_semantics=semantics,
                                vmem_limit_bytes=VMEM_LIMIT)


def _dot(a, b):
    return jnp.dot(a, b, preferred_element_type=F32)


def _dot_nt(a, b):
    return lax.dot_general(a, b, (((1,), (1,)), ((), ())),
                           preferred_element_type=F32)


def _dot_nt_f32(a, b):
    return lax.dot_general(a, b, (((1,), (1,)), ((), ())),
                           precision=lax.Precision.HIGHEST,
                           preferred_element_type=F32)


def _exp_bf16(z):
    return jnp.exp(z.astype(BF16))


def _with_ones(v):
    return jnp.concatenate([v, jnp.ones(v.shape, v.dtype)], axis=-1)


def _rope(x, cos, sin):
    return x * cos + pltpu.roll(x, LANES // 2, 1) * sin


def _mm_kernel(x_ref, w_ref, o_ref, xb_ref):
    @pl.when(pl.program_id(1) == 0)
    def _():
        xb_ref[...] = x_ref[...].astype(BF16)
    o_ref[...] = _dot(xb_ref[...], w_ref[...])


def _matmul(x, w):
    m, k = x.shape
    n = w.shape[1]
    tm, tn = min(MM_TILE_M, m), min(MM_TILE_N, n)
    return pl.pallas_call(
        _mm_kernel,
        out_shape=jax.ShapeDtypeStruct((m, n), F32),
        grid=(m // tm, n // tn),
        in_specs=[pl.BlockSpec((tm, k), lambda i, j: (i, 0)),
                  pl.BlockSpec((k, tn), lambda i, j: (0, j))],
        out_specs=pl.BlockSpec((tm, tn), lambda i, j: (i, j)),
        scratch_shapes=[pltpu.VMEM((tm, k), BF16)],
        compiler_params=_params("parallel", "arbitrary"),
        name="proj_matmul",
    )(x, w)


def _longnet_kernel(q_ref, k_ref, v_ref, cos_ref, sin_ref, o_ref,
                    qs, ks, vs, oc, lc, *, seq):
    band = DILATED_CONFIGS[0][0] // DILATED_CONFIGS[0][1]
    cos, sin = cos_ref[...], sin_ref[...]
    qs[...] = _rope(q_ref[0], cos, sin) * (HEAD_DIM ** -0.5)
    ks[...] = _rope(k_ref[0], cos, sin)
    vs[...] = v_ref[0]

    def band_mask(g, with_prev):
        width = 2 * band if with_prev else band
        qi = lax.broadcasted_iota(jnp.int32, (g, band, width), 1)
        kj = lax.broadcasted_iota(jnp.int32, (g, band, width), 2)
        if with_prev:
            return jnp.logical_and(kj >= qi, kj <= qi + band)
        return kj <= qi

    def rows(dil, start, size):
        return pl.ds(start, size) if dil == 1 else pl.ds(start, size, stride=dil)

    first_blocks, later_blocks = [], []
    for c, (window, dil) in enumerate(DILATED_CONFIGS):
        assert window // dil == band and seq % (dil * band) == 0
        for r in range(dil):
            for n in range(seq // (dil * band)):
                (later_blocks if n else first_blocks).append((c, dil, r + dil * band * n))

    def attend(group, with_prev):
        def stacked(ref, back, size):
            return jnp.stack([ref[rows(d, st - back * d * band, size), :]
                              for _, d, st in group]).astype(BF16)
        qb = stacked(qs, 0, band)
        kb = stacked(ks, 1, 2 * band) if with_prev else stacked(ks, 0, band)
        vb = stacked(vs, 1, 2 * band) if with_prev else stacked(vs, 0, band)
        s = lax.dot_general(qb, kb, (((2,), (2,)), ((0,), (0,))),
                            preferred_element_type=F32)
        s = jnp.where(band_mask(len(group), with_prev), s, MASK_VALUE)
        m = jnp.max(s, axis=-1, keepdims=True)
        acc = lax.dot_general(_exp_bf16(s - m), _with_ones(vb), (((2,), (1,)), ((0,), (0,))),
                              preferred_element_type=F32)
        l = acc[:, :, HEAD_DIM:]
        o = acc[:, :, :HEAD_DIM] / l
        lse = m + jnp.log(l[:, :, :1])
        for g, (c, d, st) in enumerate(group):
            oc[c, rows(d, st, band), :] = o[g]
            lc[c, rows(d, st, band), :] = jnp.broadcast_to(lse[g], (band, LANES))

    for blocks, with_prev in ((first_blocks, False), (later_blocks, True)):
        for i in range(0, len(blocks), LONGNET_GROUP):
            attend(blocks[i:i + LONGNET_GROUP], with_prev)

    lses = [lc[c] for c in range(len(DILATED_CONFIGS))]
    top = functools.reduce(jnp.maximum, lses)
    es = [jnp.exp(v - top) for v in lses]
    num = sum(e * oc[c] for c, e in enumerate(es))
    o_ref[0] = num / sum(es)


def _longnet(proj, cos, sin):
    b, s, _ = proj.shape
    n_cfg = len(DILATED_CONFIGS)
    blk = lambda off: pl.BlockSpec((1, s, HEAD_DIM), lambda i, h: (i, 0, off + h))
    tab = pl.BlockSpec((s, HEAD_DIM), lambda i, h: (0, 0))
    return pl.pallas_call(
        functools.partial(_longnet_kernel, seq=s),
        out_shape=jax.ShapeDtypeStruct((b, s, A_HEADS * HEAD_DIM), F32),
        grid=(b, A_HEADS),
        in_specs=[blk(0), blk(A_HEADS), blk(2 * A_HEADS), tab, tab],
        out_specs=pl.BlockSpec((1, s, HEAD_DIM), lambda i, h: (i, 0, h)),
        scratch_shapes=[pltpu.VMEM((s, HEAD_DIM), F32),
                        pltpu.VMEM((s, HEAD_DIM), F32),
                        pltpu.VMEM((s, HEAD_DIM), F32),
                        pltpu.VMEM((n_cfg, s, HEAD_DIM), F32),
                        pltpu.VMEM((n_cfg, s, LANES), F32)],
        compiler_params=_params("parallel", "parallel"),
        name="longnet_attention",
    )(proj, proj, proj, cos, sin)


def _softmax_blocks(qb, k_ref, v1_ref, blocks, tile):
    scores = []
    m = None
    for start, mask in blocks:
        s = _dot_nt(qb, k_ref[pl.ds(start, tile), :])
        if mask is not None:
            s = jnp.where(mask, s, MASK_VALUE)
        scores.append(s)
        bm = jnp.max(s, axis=-1, keepdims=True)
        m = bm if m is None else jnp.maximum(m, bm)
    acc = None
    for (start, _), s in zip(blocks, scores):
        pv = _dot(_exp_bf16(s - m), v1_ref[pl.ds(start, tile), :])
        acc = pv if acc is None else acc + pv
    half = acc.shape[1] // 2
    return acc[:, :half] / acc[:, half:]


def _moba_kernel(q_ref, k_ref, v_ref, cos_ref, sin_ref, o_ref, qs, ks, vs, *, seq):
    blk = MOBA_BLOCK
    nblk = seq // blk
    cos, sin = cos_ref[...], sin_ref[...]
    q = _rope(q_ref[0], cos, sin)
    k = _rope(k_ref[0], cos, sin)
    qs[...] = (q * (HEAD_DIM ** -0.5)).astype(BF16)
    ks[...] = k.astype(BF16)
    vs[...] = _with_ones(v_ref[0].astype(BF16))
    k_mean = jnp.concatenate(
        [jnp.mean(k[n * blk:(n + 1) * blk], axis=0, keepdims=True) for n in range(nblk)],
        axis=0)
    gate = _dot_nt_f32(q, k_mean)
    row = lax.broadcasted_iota(jnp.int32, (blk, blk), 0)
    col = lax.broadcasted_iota(jnp.int32, (blk, blk), 1)
    causal = col <= row
    n_top = min(MOBA_TOP_K, nblk)

    for i in range(nblk):
        g = gate[i * blk:(i + 1) * blk, :]
        blocks = [(i * blk, causal)]
        for n in range(i):
            if i <= n_top:
                blocks.append((n * blk, None))
                continue
            gn = g[:, n:n + 1]
            rank = jnp.zeros((blk, 1), jnp.int32)
            for o in range(i):
                if o == n:
                    continue
                go = g[:, o:o + 1]
                ahead = (go > gn) | (go == gn) if o < n else (go > gn)
                rank = rank + ahead.astype(jnp.int32)
            blocks.append((n * blk, rank < n_top))
        o_ref[0, pl.ds(i * blk, blk), :] = _softmax_blocks(
            qs[pl.ds(i * blk, blk), :], ks, vs, blocks, blk)


def _moba(proj, cos, sin):
    b, s, _ = proj.shape
    base = 3 * A_HEADS
    blk = lambda off: pl.BlockSpec((1, s, HEAD_DIM), lambda i, h: (i, 0, base + off + h))
    tab = pl.BlockSpec((s, HEAD_DIM), lambda i, h: (0, 0))
    return pl.pallas_call(
        functools.partial(_moba_kernel, seq=s),
        out_shape=jax.ShapeDtypeStruct((b, s, B_HEADS * HEAD_DIM), F32),
        grid=(b, B_HEADS),
        in_specs=[blk(0), blk(B_HEADS), blk(2 * B_HEADS), tab, tab],
        out_specs=pl.BlockSpec((1, s, HEAD_DIM), lambda i, h: (i, 0, h)),
        scratch_shapes=[pltpu.VMEM((s, HEAD_DIM), BF16)] * 2
        + [pltpu.VMEM((s, 2 * HEAD_DIM), BF16)],
        compiler_params=_params("parallel", "parallel"),
        name="moba_attention",
    )(proj, proj, proj, cos, sin)


def _layer_norm(z, g, b):
    mu = jnp.mean(z, axis=-1, keepdims=True)
    zc = z - mu
    var = jnp.mean(zc * zc, axis=-1, keepdims=True)
    return zc * lax.rsqrt(var + LN_EPS) * g + b


def _route_topk(y, rw_ref, rb_ref, e_ref, g_ref):
    y_hi = y.astype(BF16)
    y_lo = (y - y_hi.astype(F32)).astype(BF16)
    hi_both = _dot(y_hi, rw_ref[...])
    lo_hi = _dot(y_lo, rw_ref[:, :LANES])
    logits = hi_both[:, :LANES] + (hi_both[:, LANES:] + lo_hi) + rb_ref[...]
    lane = lax.broadcasted_iota(jnp.int32, logits.shape, 1)
    vals, idxs = [], []
    for _ in range(MOE_TOP_K):
        v = jnp.max(logits, axis=-1, keepdims=True)
        ix = jnp.min(jnp.where(logits == v, lane, LANES), axis=-1, keepdims=True)
        vals.append(v)
        idxs.append(ix)
        logits = jnp.where(lane == ix, -jnp.inf, logits)
    exps = [jnp.exp(v - vals[0]) for v in vals]
    denom = sum(exps)
    e_out = jnp.zeros(lane.shape, jnp.int32)
    g_out = jnp.zeros(lane.shape, F32)
    for kk in range(MOE_TOP_K):
        e_out = jnp.where(lane == kk, idxs[kk], e_out)
        g_out = jnp.where(lane == kk, exps[kk] / denom, g_out)
    e_ref[...] = e_out
    g_ref[...] = g_out


_HI16 = 0xFFFF0000


def _pack_bf16_halves(y):
    half = y.shape[1] // 2
    bits = lambda v: lax.bitcast_convert_type(v.astype(BF16).astype(F32), jnp.uint32)
    return (bits(y[:, :half]) >> 16) | (bits(y[:, half:]) & jnp.uint32(_HI16))


def _unpack_bf16_halves(pk):
    lo = lax.bitcast_convert_type(pk << 16, F32).astype(BF16)
    hi = lax.bitcast_convert_type(pk & jnp.uint32(_HI16), F32).astype(BF16)
    return lo, hi


def _outproj_ln_kernel(*refs, n_in):
    o_refs, w_refs = refs[:n_in], refs[n_in:2 * n_in]
    x_ref, g_ref, b_ref, rw_ref, rb_ref, y_ref, pk_ref, e_ref, gt_ref = refs[2 * n_in:]
    h = sum(_dot(o[...].astype(BF16), w[...]) for o, w in zip(o_refs, w_refs))
    y = _layer_norm(ALPHA * x_ref[...] + h, g_ref[...], b_ref[...])
    y_ref[...] = y
    pk_ref[...] = _pack_bf16_halves(y)
    _route_topk(y, rw_ref, rb_ref, e_ref, gt_ref)


def _outproj_ln(outs, ws, x, g, b, rw, rb):
    t, d = x.shape
    tm = LN_TILE_M
    n_in = len(outs)
    row = lambda width: pl.BlockSpec((tm, width), lambda i: (i, 0))
    full = lambda a: pl.BlockSpec(a.shape, lambda i: (0,) * a.ndim)
    return pl.pallas_call(
        functools.partial(_outproj_ln_kernel, n_in=n_in),
        out_shape=(jax.ShapeDtypeStruct((t, d), F32),
                   jax.ShapeDtypeStruct((t, d // 2), jnp.uint32),
                   jax.ShapeDtypeStruct((t, LANES), jnp.int32),
                   jax.ShapeDtypeStruct((t, LANES), F32)),
        grid=(t // tm,),
        in_specs=([row(o.shape[1]) for o in outs] + [full(w) for w in ws]
                  + [row(d), full(g), full(b), full(rw), full(rb)]),
        out_specs=(row(d), row(d // 2), row(LANES), row(LANES)),
        compiler_params=_params("parallel"),
        name="outproj_layernorm_router",
    )(*outs, *ws, x, g, b, rw, rb)


def _rms_norm(x, g):
    return x * lax.rsqrt(jnp.mean(x * x, axis=-1, keepdims=True) + RMS_EPS) * g


def _mla_down_kernel(x_ref, w_ref, qn_ref, kvn_ref, cq_ref, ckv_ref, kr_ref):
    a = _dot(x_ref[...].astype(BF16), w_ref[...])
    cq_ref[...] = _rms_norm(a[:, :MLA_Q_RANK], qn_ref[...]).astype(BF16)
    ckv_ref[...] = _rms_norm(a[:, MLA_Q_RANK:MLA_Q_RANK + MLA_KV_RANK],
                             kvn_ref[...]).astype(BF16)
    kr_ref[...] = a[:, MLA_Q_RANK + MLA_KV_RANK:]


def _mla_down(x, w_cat, q_norm, kv_norm):
    t, d = x.shape
    tm = MM_TILE_M
    row = lambda width: pl.BlockSpec((tm, width), lambda i: (i, 0))
    full = lambda a: pl.BlockSpec(a.shape, lambda i: (0,) * a.ndim)
    return pl.pallas_call(
        _mla_down_kernel,
        out_shape=(jax.ShapeDtypeStruct((t, MLA_Q_RANK), BF16),
                   jax.ShapeDtypeStruct((t, MLA_KV_RANK), BF16),
                   jax.ShapeDtypeStruct((t, LANES), F32)),
        grid=(t // tm,),
        in_specs=[row(d), full(w_cat), full(q_norm), full(kv_norm)],
        out_specs=(row(MLA_Q_RANK), row(MLA_KV_RANK), row(LANES)),
        compiler_params=_params("parallel"),
        name="mla_down_rmsnorm",
    )(x, w_cat, q_norm, kv_norm)


def _mla_attn_kernel(cq_ref, ckv_ref, kr_ref, wq_ref, wkv_ref, cos_ref, sin_ref,
                     o_ref, qs, ks, vs, *, seq):
    tile = ATT_TILE_Q
    scale = (MLA_NOPE_DIM + MLA_ROPE_DIM) ** -0.5
    cos, sin = cos_ref[...], sin_ref[...]
    q = _dot(cq_ref[0], wq_ref[...])
    kv = _dot(ckv_ref[0], wkv_ref[...])
    qs[:, :LANES] = (q[:, :LANES] * scale).astype(BF16)
    qs[:, LANES:] = (_rope(q[:, LANES:], cos, sin) * scale).astype(BF16)
    ks[:, :LANES] = kv[:, :MLA_NOPE_DIM].astype(BF16)
    ks[:, LANES:] = _rope(kr_ref[0], cos, sin).astype(BF16)
    vs[...] = _with_ones(kv[:, MLA_NOPE_DIM:].astype(BF16))
    row = lax.broadcasted_iota(jnp.int32, (tile, tile), 0)
    col = lax.broadcasted_iota(jnp.int32, (tile, tile), 1)
    causal = col <= row
    for i in range(seq // tile):
        past = i * tile
        keys = pl.ds(0, past + tile)
        s = _dot_nt(qs[pl.ds(past, tile), :], ks[keys, :])
        s_diag = jnp.where(causal, s[:, past:], MASK_VALUE)
        m = jnp.max(s_diag, axis=-1, keepdims=True)
        if i:
            m = jnp.maximum(m, jnp.max(s[:, :past], axis=-1, keepdims=True))
            p = jnp.concatenate([_exp_bf16(s[:, :past] - m), _exp_bf16(s_diag - m)], axis=1)
        else:
            p = _exp_bf16(s_diag - m)
        acc = _dot(p, vs[keys, :])
        o_ref[0, pl.ds(past, tile), :] = acc[:, :MLA_V_DIM] / acc[:, MLA_V_DIM:]


def _mla_attn(cq, ckv, kr, wq, wkv, cos, sin):
    b, s, _ = cq.shape
    per_b = lambda width: pl.BlockSpec((1, s, width), lambda i, h: (i, 0, 0))
    head_w = lambda a: pl.BlockSpec((a.shape[0], 2 * LANES), lambda i, h: (0, h))
    tab = pl.BlockSpec((s, LANES), lambda i, h: (0, 0))
    return pl.pallas_call(
        functools.partial(_mla_attn_kernel, seq=s),
        out_shape=jax.ShapeDtypeStruct((b, s, MLA_HEADS * MLA_V_DIM), F32),
        grid=(b, MLA_HEADS),
        in_specs=[per_b(MLA_Q_RANK), per_b(MLA_KV_RANK), per_b(LANES),
                  head_w(wq), head_w(wkv), tab, tab],
        out_specs=pl.BlockSpec((1, s, MLA_V_DIM), lambda i, h: (i, 0, h)),
        scratch_shapes=[pltpu.VMEM((s, 2 * LANES), BF16),
                        pltpu.VMEM((s, 2 * LANES), BF16),
                        pltpu.VMEM((s, 2 * MLA_V_DIM), BF16)],
        compiler_params=_params("parallel", "arbitrary"),
        name="mla_attention",
    )(cq, ckv, kr, wq, wkv, cos, sin)


def _issue_row_gather(idx_of, src_hbm, n, sem_rows, dst_of, alternate=False):
    def issue(c, carry):
        base = pl.multiple_of(c * DMA_UNROLL, DMA_UNROLL)
        for u in range(DMA_UNROLL):
            pltpu.make_async_copy(src_hbm.at[pl.ds(idx_of(base + u), 1)],
                                  dst_of(base, u), sem_rows).start(
                                      priority=u % 2 if alternate else 0)
        return carry
    lax.fori_loop(0, n // DMA_UNROLL, issue, 0)


def _for_row_counts(ns, tm, fn):
    for n in range(1, tm // MOE_SUB_M + 1):
        pl.when(ns == n)(functools.partial(fn, n * MOE_SUB_M))


def _moe_gu_kernel(te_ref, nu_ref, ns_ref, tok_hbm, x_hbm, wgu_ref, bgu_ref, cmp_ref,
                   o_ref, tok_smem, xbuf, xb, wgu_b, sem_idx, sem_rows):
    t, j = pl.program_id(0), pl.program_id(1)
    _, tm, half = xbuf.shape
    two_f = wgu_b.shape[1]
    sel = cmp_ref.shape[0]
    used = t < nu_ref[0]
    slot = t % 2

    def idx_copy(tile, sl):
        return pltpu.make_async_copy(
            tok_hbm.at[tile], tok_smem.at[pl.ds(pl.multiple_of(sl * tm, tm), tm)],
            sem_idx.at[sl])

    def start_gather(tile, sl):
        _issue_row_gather(lambda a: tok_smem[sl * tm + a], x_hbm,
                          ns_ref[tile] * MOE_SUB_M, sem_rows.at[sl],
                          lambda base, u: xbuf.at[sl, pl.ds(base + u, 1)])

    @pl.when(jnp.logical_and(used, j == 0))
    def _():
        @pl.when(t == 0)
        def _():
            idx_copy(0, 0).start()
            idx_copy(0, 0).wait()
            start_gather(0, 0)

            @pl.when(nu_ref[0] > 1)
            def _():
                idx_copy(1, 1).start()

        def land(rows):
            r = pl.ds(0, rows)
            pltpu.make_async_copy(x_hbm.at[r], xbuf.at[slot, r], sem_rows.at[slot]).wait()
            lo, hi = _unpack_bf16_halves(xbuf[slot, r, :])
            xb[r, :half] = lo
            xb[r, half:] = hi

        _for_row_counts(ns_ref[t], tm, land)

        @pl.when(t + 1 < nu_ref[0])
        def _():
            idx_copy(t + 1, 1 - slot).wait()
            start_gather(t + 1, 1 - slot)

            @pl.when(t + 2 < nu_ref[0])
            def _():
                idx_copy(t + 2, slot).start()

    @pl.when(jnp.logical_not(used))
    def _():
        o_ref[...] = jnp.zeros(o_ref.shape, o_ref.dtype)

    @pl.when(used)
    def _():
        def compute(rows):
            r = pl.ds(0, rows)
            x = xb[r, :]
            is_glu = (lax.broadcasted_iota(jnp.int32, (rows, sel), 1) & 1) == 0
            for c in range(0, two_f, sel):
                wgu_b[:, c:c + sel] = wgu_ref[0, 0, :, c:c + sel].astype(BF16)
                h = _dot(x, wgu_b[:, c:c + sel]) + bgu_ref[0, 0, :, c:c + sel]
                glu = jnp.minimum(h, SWIGLU_LIMIT)
                lin = jnp.clip(h, -SWIGLU_LIMIT, SWIGLU_LIMIT) + 1.0
                lin = jnp.concatenate(
                    [pltpu.roll(lin[:, v:v + LANES], LANES - 1, 1)
                     for v in range(0, sel, LANES)], axis=1)
                act = jnp.where(is_glu, glu * jax.nn.sigmoid(SWIGLU_ALPHA * glu) * lin, 0.0)
                o_ref[r, c // 2:(c + sel) // 2] = _dot(
                    act.astype(BF16), cmp_ref[...]).astype(o_ref.dtype)
            if rows < tm:
                o_ref[pl.ds(rows, tm - rows), :] = jnp.zeros(
                    (tm - rows, o_ref.shape[1]), o_ref.dtype)

        _for_row_counts(ns_ref[t], tm, compute)


def _moe_down_kernel(te_ref, nu_ref, ns_ref, a_ref, wd_ref, bd_ref, o_ref, wd_b):
    t = pl.program_id(1)
    tm = a_ref.shape[0]

    @pl.when(t >= nu_ref[0])
    def _():
        o_ref[...] = jnp.zeros(o_ref.shape, F32)

    @pl.when(t < nu_ref[0])
    def _():
        @pl.when(jnp.logical_or(t == 0, te_ref[t] != te_ref[jnp.maximum(t - 1, 0)]))
        def _():
            wd_b[...] = wd_ref[0, 0].astype(BF16)

        def compute(rows):
            r = pl.ds(0, rows)
            o_ref[r, :] = _dot(a_ref[r, :], wd_b[...]) + bd_ref[0, 0]
            if rows < tm:
                o_ref[pl.ds(rows, tm - rows), :] = jnp.zeros(
                    (tm - rows, o_ref.shape[1]), F32)

        _for_row_counts(ns_ref[t], tm, compute)


def _moe_experts(x_packed, tok_tiles, tile_expert, n_used, n_sub, w_gu, b_gu, w_down,
                 b_down, layer):
    n_tiles, tm = tok_tiles.shape
    half = x_packed.shape[1]
    d = 2 * half
    ff = w_down.shape[2]
    tf, tn = MOE_TILE_F, MOE_TILE_N
    n_f, n_n = ff // tf, d // tn
    sel = 4 * LANES
    compact = (jnp.arange(sel)[:, None] == 2 * jnp.arange(sel // 2)[None, :]).astype(BF16)

    def tile(i, nu):
        return jnp.minimum(i, nu[0] - 1)

    def chunk(i, j, nu, last):
        return jnp.where(i < nu[0], j, last)

    act = pl.pallas_call(
        _moe_gu_kernel,
        out_shape=jax.ShapeDtypeStruct((n_tiles * tm, ff), BF16),
        grid_spec=pltpu.PrefetchScalarGridSpec(
            num_scalar_prefetch=3,
            grid=(n_tiles, n_f),
            in_specs=[
                pl.BlockSpec(memory_space=pl.ANY),
                pl.BlockSpec(memory_space=pl.ANY),
                pl.BlockSpec((1, 1, d, 2 * tf), lambda i, j, te, nu, ns:
                             (layer, te[tile(i, nu)], 0, chunk(i, j, nu, n_f - 1))),
                pl.BlockSpec((1, 1, 1, 2 * tf), lambda i, j, te, nu, ns:
                             (layer, te[tile(i, nu)], 0, chunk(i, j, nu, n_f - 1))),
                pl.BlockSpec((sel, sel // 2), lambda i, j, te, nu, ns: (0, 0)),
            ],
            out_specs=pl.BlockSpec((tm, tf), lambda i, j, te, nu, ns: (i, j)),
            scratch_shapes=[pltpu.SMEM((2 * tm,), jnp.int32),
                            pltpu.VMEM((2, tm, half), jnp.uint32),
                            pltpu.VMEM((tm, d), BF16),
                            pltpu.VMEM((d, 2 * tf), BF16),
                            pltpu.SemaphoreType.DMA((2,)),
                            pltpu.SemaphoreType.DMA((2,))]),
        compiler_params=_params("arbitrary", "arbitrary"),
        name="moe_gate_up",
    )(tile_expert, n_used, n_sub, tok_tiles, x_packed, w_gu, b_gu[:, :, None, :], compact)

    return pl.pallas_call(
        _moe_down_kernel,
        out_shape=jax.ShapeDtypeStruct((n_tiles * tm, d), F32),
        grid_spec=pltpu.PrefetchScalarGridSpec(
            num_scalar_prefetch=3,
            grid=(n_n, n_tiles),
            in_specs=[
                pl.BlockSpec((tm, ff), lambda j, i, te, nu, ns: (tile(i, nu), 0)),
                pl.BlockSpec((1, 1, ff, tn), lambda j, i, te, nu, ns:
                             (layer, te[tile(i, nu)], 0, j)),
                pl.BlockSpec((1, 1, 1, tn), lambda j, i, te, nu, ns:
                             (layer, te[tile(i, nu)], 0, j)),
            ],
            out_specs=pl.BlockSpec((tm, tn), lambda j, i, te, nu, ns: (i, j)),
            scratch_shapes=[pltpu.VMEM((ff, tn), BF16)]),
        compiler_params=_params("arbitrary", "arbitrary"),
        name="moe_down",
    )(tile_expert, n_used, n_sub, act, w_down, b_down[:, :, None, :])


def _combine_kernel(pos_hbm, ys_hbm, gates_ref, x_ref, g_ref, b_ref, y_ref,
                    pos_smem, buf, sem_idx, sem_rows):
    i = pl.program_id(0)
    n = pl.num_programs(0)
    _, k, tc, _ = buf.shape
    slot = i % 2

    def idx_copy(tile, sl):
        return pltpu.make_async_copy(
            pos_hbm.at[tile], pos_smem.at[pl.ds(pl.multiple_of(sl * k * tc, k * tc), k * tc)],
            sem_idx.at[sl])

    def start_gather(sl):
        _issue_row_gather(
            lambda a: pos_smem[sl * k * tc + a], ys_hbm, k * tc, sem_rows.at[sl],
            lambda base, u: buf.at[sl, base // tc,
                                   pl.ds(pl.multiple_of(base % tc, DMA_UNROLL) + u, 1)],
            alternate=True)

    @pl.when(i == 0)
    def _():
        idx_copy(0, 0).start()
        idx_copy(0, 0).wait()
        start_gather(0)

        @pl.when(n > 1)
        def _():
            idx_copy(1, 1).start()

    for kk in range(k):
        pltpu.make_async_copy(ys_hbm.at[pl.ds(0, tc)], buf.at[slot, kk],
                              sem_rows.at[slot]).wait()

    @pl.when(i + 1 < n)
    def _():
        idx_copy(i + 1, 1 - slot).wait()
        start_gather(1 - slot)

        @pl.when(i + 2 < n)
        def _():
            idx_copy(i + 2, slot).start()

    gates = gates_ref[...]
    h = sum(gates[:, kk:kk + 1] * buf[slot, kk] for kk in range(k))
    y_ref[...] = _layer_norm(ALPHA * x_ref[...] + h, g_ref[...], b_ref[...])


def _combine_ln(pos_tiles, ys, gates, x, g, b):
    t, d = x.shape
    tc = COMBINE_TILE
    row = lambda width: pl.BlockSpec((tc, width), lambda i: (i, 0))
    full = lambda a: pl.BlockSpec(a.shape, lambda i: (0,) * a.ndim)
    anyspec = pl.BlockSpec(memory_space=pl.ANY)
    return pl.pallas_call(
        _combine_kernel,
        out_shape=jax.ShapeDtypeStruct((t, d), F32),
        grid=(t // tc,),
        in_specs=[anyspec, anyspec, row(LANES), row(d), full(g), full(b)],
        out_specs=row(d),
        scratch_shapes=[pltpu.SMEM((2 * MOE_TOP_K * tc,), jnp.int32),
                        pltpu.VMEM((2, MOE_TOP_K, tc, d), F32),
                        pltpu.SemaphoreType.DMA((2,)),
                        pltpu.SemaphoreType.DMA((2,))],
        compiler_params=_params("arbitrary"),
        name="moe_combine_layernorm",
    )(pos_tiles, ys, gates, x, g, b)


def _rope_tables(seq, dim, slab):
    half = dim // 2
    pos = jnp.arange(seq, dtype=F32)
    inv = 1.0 / (ROPE_THETA ** (jnp.arange(0, dim, 2, dtype=F32) / dim))
    ang = pos[:, None] * inv[None, :]
    pad = jnp.zeros((seq, slab // 2 - half), F32)
    cos = jnp.concatenate([jnp.cos(ang), pad, jnp.cos(ang), pad], axis=1)
    sin = jnp.concatenate([-jnp.sin(ang), pad, jnp.sin(ang), pad], axis=1)
    return cos, sin


def _pad_rope_cols(w):
    half = MLA_ROPE_DIM // 2
    z = jnp.zeros(w.shape[:-1] + (LANES // 2 - half,), w.dtype)
    return jnp.concatenate([w[..., :half], z, w[..., half:], z], axis=-1)


def _route(top_e, tm, sub):
    t = top_e.shape[0]
    i32 = jnp.int32
    flat_e = top_e.reshape(-1)
    n_assign = t * MOE_TOP_K
    n_tiles = n_assign // tm + N_EXPERTS
    order = jnp.argsort(flat_e).astype(i32)
    sorted_e = flat_e[order]
    counts = jnp.sum(flat_e[:, None] == jnp.arange(N_EXPERTS, dtype=i32)[None, :],
                     axis=0, dtype=i32)
    padded = (counts + tm - 1) // tm * tm
    start = jnp.cumsum(counts) - counts
    pad_end = jnp.cumsum(padded)
    pad_start = pad_end - padded
    tile_start = jnp.arange(n_tiles, dtype=i32) * tm
    tile_expert = jnp.minimum(
        jnp.sum(pad_end[None, :] <= tile_start[:, None], axis=1, dtype=i32), N_EXPERTS - 1)
    n_used = (pad_end[-1:] // tm).astype(i32)
    first = tile_start - pad_start[tile_expert]
    tile_rows = jnp.clip(counts[tile_expert] - first, 0, tm)
    n_sub = (tile_rows + sub - 1) // sub
    k_in_e = first[:, None] + jnp.arange(tm, dtype=i32)[None, :]
    valid = k_in_e < counts[tile_expert][:, None]
    src = jnp.clip(start[tile_expert][:, None] + k_in_e, 0, n_assign - 1)
    row_token = jnp.where(valid, order[src] // MOE_TOP_K, 0).astype(i32)
    dest = (pad_start[sorted_e] + jnp.arange(n_assign, dtype=i32) - start[sorted_e]).astype(i32)
    _, pos = lax.sort_key_val(order, dest)
    return row_token, pos, tile_expert, n_used, n_sub.astype(i32)


def _moe_layer(x_packed, top_e, w_gu, b_gu, w_down, b_down, layer):
    tok_tiles, pos, tile_expert, n_used, n_sub = _route(top_e, MOE_TILE_M, MOE_SUB_M)
    ys = _moe_experts(x_packed, tok_tiles, tile_expert, n_used, n_sub,
                      w_gu, b_gu, w_down, b_down, layer)
    pos_tiles = pos.reshape(-1, COMBINE_TILE, MOE_TOP_K).transpose(0, 2, 1)
    return ys, pos_tiles.reshape(-1, MOE_TOP_K * COMBINE_TILE)


def _router_params(router_w, router_b):
    rw = jnp.pad(router_w, ((0, 0), (0, LANES - N_EXPERTS)))
    rw_hi = lax.reduce_precision(rw, exponent_bits=8, mantissa_bits=7)
    rw_lo = rw - rw_hi
    rb = jnp.pad(router_b, (0, LANES - N_EXPERTS), constant_values=MASK_VALUE)
    return jnp.concatenate([rw_hi, rw_lo], axis=1).astype(BF16), rb[None, :]


def kernel(x, ln_mix_g, ln_mix_b, ln_ffn_g, ln_ffn_b, w_in_ab, w_out_ab,
           mla_w_dq, mla_q_norm, mla_w_uq, mla_w_dkv, mla_kv_norm, mla_w_ukv, mla_w_out,
           router_w, router_b, moe_w_gu, moe_b_gu, moe_w_down, moe_b_down):
    b, s, d = x.shape
    t = b * s
    xt = x.reshape(t, d)
    cos_h, sin_h = _rope_tables(s, HEAD_DIM, LANES)
    cos_r, sin_r = _rope_tables(s, MLA_ROPE_DIM, LANES)
    row2 = lambda v: v[None, :]

    for layer in range(DEPTH):
        i = layer // 2
        rw, rb = _router_params(router_w[layer], router_b[layer])
        if layer % 2 == 0:
            proj = _matmul(xt, w_in_ab[i].astype(BF16)).reshape(b, s, -1)
            o_a = _longnet(proj, cos_h, sin_h).reshape(t, -1)
            o_b = _moba(proj, cos_h, sin_h).reshape(t, -1)
            w_out = w_out_ab[i].astype(BF16)
            split = A_HEADS * HEAD_DIM
            outs, ws = [o_a, o_b], [w_out[:split], w_out[split:]]
        else:
            w_cat = jnp.concatenate(
                [mla_w_dq[i], mla_w_dkv[i][:, :MLA_KV_RANK],
                 _pad_rope_cols(mla_w_dkv[i][:, MLA_KV_RANK:])], axis=1).astype(BF16)
            cq, ckv, kr = _mla_down(xt, w_cat, row2(mla_q_norm[i]), row2(mla_kv_norm[i]))
            wq = mla_w_uq[i].reshape(MLA_Q_RANK, MLA_HEADS, MLA_NOPE_DIM + MLA_ROPE_DIM)
            wq = jnp.concatenate([wq[..., :MLA_NOPE_DIM],
                                  _pad_rope_cols(wq[..., MLA_NOPE_DIM:])], axis=-1)
            wq = wq.reshape(MLA_Q_RANK, MLA_HEADS * 2 * LANES).astype(BF16)
            o = _mla_attn(cq.reshape(b, s, -1), ckv.reshape(b, s, -1), kr.reshape(b, s, -1),
                          wq, mla_w_ukv[i].astype(BF16), cos_r, sin_r)
            outs, ws = [o.reshape(t, -1)], [mla_w_out[i].astype(BF16)]
        xt, x_packed, top_e, gates = _outproj_ln(outs, ws, xt, row2(ln_mix_g[layer]),
                                                 row2(ln_mix_b[layer]), rw, rb)
        ys, pos_tiles = _moe_layer(x_packed, top_e[:, :MOE_TOP_K], moe_w_gu, moe_b_gu,
                                   moe_w_down, moe_b_down, layer)
        xt = _combine_ln(pos_tiles, ys, gates, xt,
                         row2(ln_ffn_g[layer]), row2(ln_ffn_b[layer]))
    return xt.reshape(b, s, d)
```
